```python
import math
import jax, jax.numpy as jnp
from jax import lax
import numpy as np

D_MODEL = 2048
BATCH = 2
SEQ = 4096
DEPTH = 4

N_MIXERS = 2
N_FOX_LAYERS = (DEPTH + 1) // 2
N_RET_LAYERS = DEPTH // 2

FOX_HEAD_DIM = 128
FOX_HEADS = D_MODEL // FOX_HEAD_DIM
FOX_WIDTH = FOX_HEADS * FOX_HEAD_DIM
FOX_BLOCK = 128
FOX_IN_COLS = 4 * FOX_WIDTH + FOX_HEADS

RET_QK_DIM = 256
RET_HEADS = D_MODEL // RET_QK_DIM
RET_V_DIM = 2 * RET_QK_DIM
RET_QK_WIDTH = RET_HEADS * RET_QK_DIM
RET_V_WIDTH = RET_HEADS * RET_V_DIM
RET_CHUNK = 128
RET_IN_COLS = 2 * RET_QK_WIDTH + 2 * RET_V_WIDTH
ROPE_BASE = 10000.0

LN_EPS = 1e-5
GN_EPS = 1e-6
QK_EPS = 1e-6
DEEPNORM_ALPHA = (2.0 * DEPTH) ** 0.25
DEEPNORM_BETA = (8.0 * DEPTH) ** -0.25

kernel_name = "fox_retnet_interleaved_deepnorm"


def layer_norm(x, g, b):
    xf = x.astype(jnp.float32)
    mu = jnp.mean(xf, axis=-1, keepdims=True)
    var = jnp.mean(jnp.square(xf - mu), axis=-1, keepdims=True)
    y = (xf - mu) * lax.rsqrt(var + LN_EPS) * g.astype(jnp.float32) + b.astype(jnp.float32)
    return y.astype(x.dtype)


def rms_norm(x, g):
    xf = x.astype(jnp.float32)
    y = xf * lax.rsqrt(jnp.mean(jnp.square(xf), axis=-1, keepdims=True) + QK_EPS) * g.astype(jnp.float32)
    return y.astype(x.dtype)


def group_norm_heads(x):
    mu = jnp.mean(x, axis=-1, keepdims=True)
    var = jnp.mean(jnp.square(x - mu), axis=-1, keepdims=True)
    return (x - mu) * lax.rsqrt(var + GN_EPS)


def rotary(t):
    S, d = t.shape[1], t.shape[-1]
    inv_freq = ROPE_BASE ** (-jnp.arange(0, d, 2, dtype=jnp.float32) / d)
    ang = jnp.arange(S, dtype=jnp.float32)[:, None] * inv_freq[None, :]
    cos = jnp.cos(ang)[None, :, None, :]
    sin = jnp.sin(ang)[None, :, None, :]
    t1, t2 = t[..., : d // 2], t[..., d // 2:]
    return jnp.concatenate([t1 * cos - t2 * sin, t1 * sin + t2 * cos], axis=-1)


def fox_branch(x, w_in, b_f, q_gain, k_gain, w_out):
    B, S, _ = x.shape
    proj = jnp.einsum('bsd,de->bse', x, w_in)
    q, k, v, gate, f_logit = jnp.split(
        proj, [FOX_WIDTH, 2 * FOX_WIDTH, 3 * FOX_WIDTH, 4 * FOX_WIDTH], axis=-1)

    def heads(t):
        return t.reshape(B, S, FOX_HEADS, FOX_HEAD_DIM).transpose(0, 2, 1, 3)

    q = rms_norm(heads(q), q_gain)
    k = rms_norm(heads(k), k_gain)
    v = heads(v)
    log_f = jax.nn.log_sigmoid((f_logit + b_f).astype(jnp.float32))
    c = jnp.cumsum(log_f, axis=1).transpose(0, 2, 1)

    n_blocks = S // FOX_BLOCK
    q_blocks = q.reshape(B, FOX_HEADS, n_blocks, FOX_BLOCK, FOX_HEAD_DIM).transpose(2, 0, 1, 3, 4)
    c_blocks = c.reshape(B, FOX_HEADS, n_blocks, FOX_BLOCK).transpose(2, 0, 1, 3)
    key_pos = jnp.arange(S)
    scale = FOX_HEAD_DIM ** -0.5

    def attend_block(args):
        qb, cb, blk = args
        s = jnp.einsum('bhqd,bhkd->bhqk', qb, k).astype(jnp.float32) * scale
        s = s + cb[..., :, None] - c[..., None, :]
        q_pos = blk * FOX_BLOCK + jnp.arange(FOX_BLOCK)
        s = jnp.where(key_pos[None, :] <= q_pos[:, None], s, -jnp.inf)
        p = jax.nn.softmax(s, axis=-1).astype(v.dtype)
        return jnp.einsum('bhqk,bhkd->bhqd', p, v)

    o = lax.map(attend_block, (q_blocks, c_blocks, jnp.arange(n_blocks)))
    o = o.transpose(1, 0, 3, 2, 4).reshape(B, S, FOX_WIDTH)
    y = o * jax.nn.silu(gate)
    return jnp.einsum('bse,ed->bsd', y, w_out)


def retention_branch(x, w_in, w_out):
    B, S, _ = x.shape
    proj = jnp.einsum('bsd,de->bse', x, w_in)
    q, k, v, gate = jnp.split(
        proj, [RET_QK_WIDTH, 2 * RET_QK_WIDTH, 2 * RET_QK_WIDTH + RET_V_WIDTH], axis=-1)
    q = rotary(q.reshape(B, S, RET_HEADS, RET_QK_DIM).astype(jnp.float32))
    k = rotary(k.reshape(B, S, RET_HEADS, RET_QK_DIM).astype(jnp.float32)) * (RET_QK_DIM ** -0.5)
    v = v.reshape(B, S, RET_HEADS, RET_V_DIM).astype(jnp.float32)

    log_gamma = jnp.log1p(-jnp.exp2(-5.0 - jnp.arange(RET_HEADS, dtype=jnp.float32)))
    pos = jnp.arange(RET_CHUNK, dtype=jnp.float32)
    diff = pos[:, None] - pos[None, :]
    intra_decay = jnp.where(diff >= 0,
                            jnp.exp(jnp.maximum(diff, 0.0)[None] * log_gamma[:, None, None]),
                            0.0)
    query_decay = jnp.exp((pos[None, :] + 1.0) * log_gamma[:, None])
    key_decay = jnp.exp((RET_CHUNK - 1.0 - pos[None, :]) * log_gamma[:, None])
    chunk_decay = jnp.exp(RET_CHUNK * log_gamma)

    n_chunks = S // RET_CHUNK

    def chunks(t):
        return t.reshape(B, n_chunks, RET_CHUNK, RET_HEADS, t.shape[-1]).transpose(1, 0, 3, 2, 4)

    def step(R, inp):
        qc, kc, vc = inp
        inner = jnp.einsum('bhqd,bhkd->bhqk', qc, kc) * intra_decay[None]
        o = (jnp.einsum('bhqk,bhkv->bhqv', inner, vc)
             + jnp.einsum('bhqd,bhdv->bhqv', qc, R) * query_decay[None, :, :, None])
        R = (R * chunk_decay[None, :, None, None]
             + jnp.einsum('bhkd,bhkv->bhdv', kc * key_decay[None, :, :, None], vc))
        return R, o

    R0 = jnp.zeros((B, RET_HEADS, RET_QK_DIM, RET_V_DIM), jnp.float32)
    _, o = lax.scan(step, R0, (chunks(q), chunks(k), chunks(v)))
    o = o.transpose(1, 0, 3, 2, 4)
    o = group_norm_heads(o).reshape(B, S, RET_V_WIDTH).astype(gate.dtype)
    y = o * jax.nn.silu(gate)
    return jnp.einsum('bse,ed->bsd', y, w_out)


def setup_inputs(seed: int = 0) -> dict:
    key = jax.random.key(seed)
    ks = jax.random.split(key, 10)
    std_in = D_MODEL ** -0.5
    fox_col_scale = jnp.ones((FOX_IN_COLS,), jnp.float32).at[2 * FOX_WIDTH:3 * FOX_WIDTH].set(DEEPNORM_BETA)
    ret_col_scale = jnp.ones((RET_IN_COLS,), jnp.float32).at[
        2 * RET_QK_WIDTH:2 * RET_QK_WIDTH + RET_V_WIDTH].set(DEEPNORM_BETA)
    x = jax.random.normal(ks[0], (BATCH, SEQ, D_MODEL), jnp.float32)
    fox_w_in = jax.random.normal(ks[1], (N_FOX_LAYERS, D_MODEL, FOX_IN_COLS), jnp.float32) * std_in * fox_col_scale
    fox_b_f = 3.0 + 0.5 * jax.random.normal(ks[2], (N_FOX_LAYERS, FOX_HEADS), jnp.float32)
    fox_q_gain = 1.0 + 0.02 * jax.random.normal(ks[3], (N_FOX_LAYERS, FOX_HEAD_DIM), jnp.float32)
    fox_k_gain = 1.0 + 0.02 * jax.random.normal(ks[4], (N_FOX_LAYERS, FOX_HEAD_DIM), jnp.float32)
    fox_w_out = (jax.random.normal(ks[5], (N_FOX_LAYERS, FOX_WIDTH, D_MODEL), jnp.float32)
                 * (FOX_WIDTH ** -0.5) * DEEPNORM_BETA)
    ret_w_in = jax.random.normal(ks[6], (N_RET_LAYERS, D_MODEL, RET_IN_COLS), jnp.float32) * std_in * ret_col_scale
    ret_w_out = (jax.random.normal(ks[7], (N_RET_LAYERS, RET_V_WIDTH, D_MODEL), jnp.float32)
                 * (RET_V_WIDTH ** -0.5) * DEEPNORM_BETA)
    ln_gain = 1.0 + 0.02 * jax.random.normal(ks[8], (DEPTH, D_MODEL), jnp.float32)
    ln_bias = 0.02 * jax.random.normal(ks[9], (DEPTH, D_MODEL), jnp.float32)
    return {"x": x, "fox_w_in": fox_w_in, "fox_b_f": fox_b_f, "fox_q_gain": fox_q_gain,
            "fox_k_gain": fox_k_gain, "fox_w_out": fox_w_out, "ret_w_in": ret_w_in,
            "ret_w_out": ret_w_out, "ln_gain": ln_gain, "ln_bias": ln_bias}


def reference(x, fox_w_in, fox_b_f, fox_q_gain, fox_k_gain, fox_w_out, ret_w_in, ret_w_out,
              ln_gain, ln_bias):
    h = x
    for i in range(DEPTH):
        j = i // N_MIXERS
        if i % N_MIXERS == 0:
            y = fox_branch(h, fox_w_in[j], fox_b_f[j], fox_q_gain[j], fox_k_gain[j], fox_w_out[j])
        else:
            y = retention_branch(h, ret_w_in[j], ret_w_out[j])
        h = layer_norm(DEEPNORM_ALPHA * h + y, ln_gain[i], ln_bias[i])
    return h
```

```python
import functools
import math

import jax
import jax.numpy as jnp
from jax import lax
from jax.experimental import pallas as pl
from jax.experimental.pallas import tpu as pltpu

F32 = jnp.float32
BF16 = jnp.bfloat16

LANES = 128

FOX_HEAD_DIM = 128
RET_QK_DIM = 256
RET_V_DIM = 512
ROPE_BASE = 10000.0
LN_EPS = 1e-5
GN_EPS = 1e-6
QK_EPS = 1e-6

VMEM_LIMIT = 56 * 1024 * 1024


def _cparams(sem):
    return pltpu.CompilerParams(dimension_semantics=sem, vmem_limit_bytes=VMEM_LIMIT)


def _silu(g):
    return g * (1.0 / (1.0 + jnp.exp(-g)))


def _fox_in_kernel(x_ref, w_ref, qg_ref, kg_ref, o_ref, *, tn, width, q_scale):
    j = pl.program_id(1)
    acc = jnp.dot(x_ref[...], w_ref[...], preferred_element_type=F32)
    per_region = width // tn

    def rms(g_ref, scale):
        for hh in range(tn // FOX_HEAD_DIM):
            sl = slice(hh * FOX_HEAD_DIM, (hh + 1) * FOX_HEAD_DIM)
            a = acc[:, sl]
            ms = jnp.mean(a * a, axis=-1, keepdims=True)
            y = a * lax.rsqrt(ms + QK_EPS) * g_ref[...]
            if scale is not None:
                y = y * scale
            o_ref[:, sl] = y.astype(o_ref.dtype)

    @pl.when(j < per_region)
    def _():
        rms(qg_ref, q_scale)

    @pl.when((j >= per_region) & (j < 2 * per_region))
    def _():
        rms(kg_ref, None)

    @pl.when((j >= 2 * per_region) & (j < 3 * per_region))
    def _():
        o_ref[...] = acc.astype(o_ref.dtype)

    @pl.when(j >= 3 * per_region)
    def _():
        o_ref[...] = _silu(acc).astype(o_ref.dtype)


def _fox_in_proj(x_bf, w_bf, q_gain, k_gain, *, tm=1024, tn=512):
    m, d = x_bf.shape
    n = w_bf.shape[1]
    width = n // 4
    kern = functools.partial(_fox_in_kernel, tn=tn, width=width,
                             q_scale=FOX_HEAD_DIM ** -0.5)
    return pl.pallas_call(
        kern,
        grid=(m // tm, n // tn),
        in_specs=[
            pl.BlockSpec((tm, d), lambda i, j: (i, 0)),
            pl.BlockSpec((d, tn), lambda i, j: (0, j)),
            pl.BlockSpec((1, FOX_HEAD_DIM), lambda i, j: (0, 0)),
            pl.BlockSpec((1, FOX_HEAD_DIM), lambda i, j: (0, 0)),
        ],
        out_specs=pl.BlockSpec((tm, tn), lambda i, j: (i, j)),
        out_shape=jax.ShapeDtypeStruct((m, n), BF16),
        compiler_params=_cparams(("parallel", "arbitrary")),
        name="fox_in_proj",
    )(x_bf, w_bf, q_gain.reshape(1, -1), k_gain.reshape(1, -1))


def _split3(v):
    hi = v.astype(BF16)
    r = v - hi.astype(F32)
    mid = r.astype(BF16)
    lo = (r - mid.astype(F32)).astype(BF16)
    return hi, mid, lo


def _fgate_kernel(x_ref, w_ref, b_ref, c_ref, carry_ref):
    ts = x_ref.shape[0]

    @pl.when(pl.program_id(1) == 0)
    def _():
        carry_ref[...] = jnp.zeros_like(carry_ref)

    z = jnp.dot(x_ref[...], w_ref[...], preferred_element_type=F32) + b_ref[...]
    log_f = jnp.minimum(z, 0.0) - jnp.log1p(jnp.exp(-jnp.abs(z)))
    row = lax.broadcasted_iota(jnp.int32, (ts, ts), 0)
    col = lax.broadcasted_iota(jnp.int32, (ts, ts), 1)
    tri = jnp.where(col <= row, 1.0, 0.0).astype(BF16)
    cs = carry_ref[...]
    for part in _split3(log_f):
        cs = cs + jnp.dot(tri, part, preferred_element_type=F32)
    c_ref[...] = cs
    carry_ref[...] = cs[ts - 1:ts, :]


def _fox_forget_cumsum(x_bf3, wf_bf, b_f, *, ts=512):
    b, s, d = x_bf3.shape
    h = wf_bf.shape[1]
    return pl.pallas_call(
        _fgate_kernel,
        grid=(b, s // ts),
        in_specs=[
            pl.BlockSpec((None, ts, d), lambda bi, si: (bi, si, 0)),
            pl.BlockSpec((d, h), lambda bi, si: (0, 0)),
            pl.BlockSpec((1, h), lambda bi, si: (0, 0)),
        ],
        out_specs=pl.BlockSpec((None, ts, h), lambda bi, si: (bi, si, 0)),
        out_shape=jax.ShapeDtypeStruct((b, s, h), F32),
        scratch_shapes=[pltpu.VMEM((1, h), F32)],
        compiler_params=_cparams(("parallel", "arbitrary")),
        name="fox_forget_cumsum",
    )(x_bf3, wf_bf, b_f.reshape(1, -1))


def _fox_attn_kernel(q_ref, k_ref, v_ref, g_ref, c_ref, o_ref, kaug_ref, qx_ref, *, tq):
    h = pl.program_id(1)
    i = pl.program_id(2)
    s_len = k_ref.shape[0]
    n_heads = c_ref.shape[1]
    dh = FOX_HEAD_DIM

    @pl.when(i == 0)
    def _():
        parts = _split3(c_ref[...])
        row = lax.broadcasted_iota(jnp.int32, (n_heads, LANES), 0)
        col = lax.broadcasted_iota(jnp.int32, (n_heads, LANES), 1)
        xq = jnp.zeros((s_len, LANES), F32)
        xk = jnp.zeros((s_len, LANES), F32)
        for t, part in enumerate(parts):
            sel_q = jnp.where((row == h) & (col == t), 1.0, 0.0).astype(BF16)
            sel_k = jnp.where((row == h) & (col == 3 + t), -1.0, 0.0).astype(BF16)
            xq = xq + jnp.dot(part, sel_q, preferred_element_type=F32)
            xk = xk + jnp.dot(part, sel_k, preferred_element_type=F32)
        lane = lax.broadcasted_iota(jnp.int32, (s_len, LANES), 1)
        qx_ref[...] = (xq + jnp.where((lane >= 3) & (lane < 6), 1.0, 0.0)).astype(BF16)
        kaug_ref[:, :dh] = k_ref[...]
        kaug_ref[:, dh:] = (xk + jnp.where(lane < 3, 1.0, 0.0)).astype(BF16)

    q0 = pl.multiple_of(i * tq, tq)
    qa = jnp.concatenate([q_ref[...], qx_ref[pl.ds(q0, tq), :]], axis=1)

    def block(j, carry, masked):
        m, l, acc = carry
        k0 = pl.multiple_of(j * tq, tq)
        kb = kaug_ref[pl.ds(k0, tq), :]
        s = lax.dot_general(qa, kb, (((1,), (1,)), ((), ())),
                            preferred_element_type=F32)
        if masked:
            row = lax.broadcasted_iota(jnp.int32, (tq, tq), 0)
            col = lax.broadcasted_iota(jnp.int32, (tq, tq), 1)
            s = jnp.where(col <= row, s, -jnp.inf)
        m_new = jnp.maximum(m, jnp.max(s, axis=-1, keepdims=True))
        alpha = jnp.exp(m - m_new)
        p = jnp.exp(s - m_new)
        l = alpha * l + jnp.sum(p, axis=-1, keepdims=True)
        pv = jnp.dot(p.astype(BF16), v_ref[pl.ds(k0, tq), :], preferred_element_type=F32)
        return m_new, l, alpha * acc + pv

    init = (jnp.full((tq, 1), -jnp.inf, F32), jnp.zeros((tq, 1), F32),
            jnp.zeros((tq, dh), F32))
    carry = lax.fori_loop(0, i, lambda j, c: block(j, c, False), init)
    _, l, acc = block(i, carry, True)
    o_ref[...] = (acc / l * g_ref[...].astype(F32)).astype(o_ref.dtype)


def _fox_attention(proj, c, batch, seq, *, tq=256):
    m, n = proj.shape
    width = n // 4
    heads = width // FOX_HEAD_DIM
    nq = seq // tq
    dh = FOX_HEAD_DIM
    kern = functools.partial(_fox_attn_kernel, tq=tq)
    return pl.pallas_call(
        kern,
        grid=(batch, heads, nq),
        in_specs=[
            pl.BlockSpec((tq, dh), lambda b, h, i: (b * nq + i, h)),
            pl.BlockSpec((seq, dh), lambda b, h, i: (b, heads + h)),
            pl.BlockSpec((seq, dh), lambda b, h, i: (b, 2 * heads + h)),
            pl.BlockSpec((tq, dh), lambda b, h, i: (b * nq + i, 3 * heads + h)),
            pl.BlockSpec((None, seq, heads), lambda b, h, i: (b, 0, 0)),
        ],
        out_specs=pl.BlockSpec((tq, dh), lambda b, h, i: (b * nq + i, h)),
        out_shape=jax.ShapeDtypeStruct((m, width), BF16),
        scratch_shapes=[pltpu.VMEM((seq, 2 * dh), BF16), pltpu.VMEM((seq, LANES), BF16)],
        compiler_params=_cparams(("parallel", "parallel", "arbitrary")),
        name="fox_attention",
    )(proj, proj, proj, proj, c)


def _out_ln_kernel(y_ref, w_ref, h_ref, g_ref, b_ref, o_ref, obf_ref, *, alpha):
    br = jnp.dot(y_ref[...], w_ref[...], preferred_element_type=F32)
    z = alpha * h_ref[...] + br
    mu = jnp.mean(z, axis=-1, keepdims=True)
    zc = z - mu
    var = jnp.mean(zc * zc, axis=-1, keepdims=True)
    out = zc * lax.rsqrt(var + LN_EPS) * g_ref[...] + b_ref[...]
    o_ref[...] = out
    obf_ref[...] = out.astype(obf_ref.dtype)


def _out_proj_ln(y_bf, w_bf, h, gain, bias, alpha, *, tm=256):
    m, k = y_bf.shape
    d = w_bf.shape[1]
    kern = functools.partial(_out_ln_kernel, alpha=alpha)
    return pl.pallas_call(
        kern,
        grid=(m // tm,),
        in_specs=[
            pl.BlockSpec((tm, k), lambda i: (i, 0)),
            pl.BlockSpec((k, d), lambda i: (0, 0)),
            pl.BlockSpec((tm, d), lambda i: (i, 0)),
            pl.BlockSpec((1, d), lambda i: (0, 0)),
            pl.BlockSpec((1, d), lambda i: (0, 0)),
        ],
        out_specs=[pl.BlockSpec((tm, d), lambda i: (i, 0)),
                   pl.BlockSpec((tm, d), lambda i: (i, 0))],
        out_shape=[jax.ShapeDtypeStruct((m, d), F32), jax.ShapeDtypeStruct((m, d), BF16)],
        compiler_params=_cparams(("parallel",)),
        name="out_proj_ln",
    )(y_bf, w_bf, h, gain.reshape(1, -1), bias.reshape(1, -1))


def _ret_in_kernel(x_ref, w_ref, cos_ref, sin_ref, o_ref, *, tn, qk_width, v_width, k_scale):
    j = pl.program_id(1)
    acc = jnp.dot(x_ref[...], w_ref[...], preferred_element_type=F32)
    half = RET_QK_DIM // 2
    nq = qk_width // tn
    nv = v_width // tn

    def rope(scale):
        cos = cos_ref[...]
        sin = sin_ref[...]
        for hh in range(tn // RET_QK_DIM):
            base = hh * RET_QK_DIM
            t1 = acc[:, base:base + half]
            t2 = acc[:, base + half:base + RET_QK_DIM]
            r1 = t1 * cos - t2 * sin
            r2 = t1 * sin + t2 * cos
            if scale is not None:
                r1 = r1 * scale
                r2 = r2 * scale
            o_ref[:, base:base + half] = r1.astype(o_ref.dtype)
            o_ref[:, base + half:base + RET_QK_DIM] = r2.astype(o_ref.dtype)

    @pl.when(j < nq)
    def _():
        rope(None)

    @pl.when((j >= nq) & (j < 2 * nq))
    def _():
        rope(k_scale)

    @pl.when((j >= 2 * nq) & (j < 2 * nq + nv))
    def _():
        o_ref[...] = acc.astype(o_ref.dtype)

    @pl.when(j >= 2 * nq + nv)
    def _():
        o_ref[...] = _silu(acc).astype(o_ref.dtype)


def _ret_in_proj(x_bf, w_bf, cos, sin, seq, qk_width, v_width, *, tm=1024, tn=512):
    m, d = x_bf.shape
    n = w_bf.shape[1]
    half = RET_QK_DIM // 2
    pos_blocks = seq // tm
    kern = functools.partial(_ret_in_kernel, tn=tn, qk_width=qk_width, v_width=v_width,
                             k_scale=RET_QK_DIM ** -0.5)
    return pl.pallas_call(
        kern,
        grid=(m // tm, n // tn),
        in_specs=[
            pl.BlockSpec((tm, d), lambda i, j: (i, 0)),
            pl.BlockSpec((d, tn), lambda i, j: (0, j)),
            pl.BlockSpec((tm, half), lambda i, j: (i % pos_blocks, 0)),
            pl.BlockSpec((tm, half), lambda i, j: (i % pos_blocks, 0)),
        ],
        out_specs=pl.BlockSpec((tm, tn), lambda i, j: (i, j)),
        out_shape=jax.ShapeDtypeStruct((m, n), BF16),
        compiler_params=_cparams(("parallel", "arbitrary")),
        name="ret_in_proj",
    )(x_bf, w_bf, cos, sin)


def _retention_kernel(lg_ref, q_ref, k_ref, v_ref, g_ref, o_ref, r_ref, *, chunk):
    h = pl.program_id(1)
    n = pl.program_id(2)

    @pl.when(n == 0)
    def _():
        r_ref[...] = jnp.zeros_like(r_ref)

    lg = lg_ref[h]
    row = lax.broadcasted_iota(jnp.int32, (chunk, chunk), 0)
    col = lax.broadcasted_iota(jnp.int32, (chunk, chunk), 1)
    diff = (row - col).astype(F32)
    intra = jnp.where(diff >= 0, jnp.exp(jnp.maximum(diff, 0.0) * lg), 0.0)
    pos = lax.broadcasted_iota(jnp.int32, (chunk, 1), 0).astype(F32)
    q_decay = jnp.exp((pos + 1.0) * lg)
    k_decay = jnp.exp((chunk - 1.0 - pos) * lg)
    c_decay = jnp.exp(jnp.full((1, 1), chunk, F32) * lg)

    q = q_ref[...]
    k = k_ref[...]
    v = v_ref[...]
    r = r_ref[...]
    inner = lax.dot_general(q, k, (((1,), (1,)), ((), ())), preferred_element_type=F32) * intra
    o = (jnp.dot(inner.astype(BF16), v, preferred_element_type=F32)
         + jnp.dot(q, r.astype(BF16), preferred_element_type=F32) * q_decay)
    kd = (k.astype(F32) * k_decay).astype(BF16)
    r_ref[...] = r * c_decay + lax.dot_general(kd, v, (((0,), (0,)), ((), ())),
                                               preferred_element_type=F32)

    mu = jnp.mean(o, axis=-1, keepdims=True)
    oc = o - mu
    var = jnp.mean(oc * oc, axis=-1, keepdims=True)
    o_ref[...] = (oc * lax.rsqrt(var + GN_EPS) * g_ref[...].astype(F32)).astype(o_ref.dtype)


def _retention(proj, log_gamma, batch, seq, heads, *, chunk=256):
    m = proj.shape[0]
    nc = seq // chunk
    dk, dv = RET_QK_DIM, RET_V_DIM
    v_off = 2 * heads * dk // dv
    g_off = v_off + heads
    kern = functools.partial(_retention_kernel, chunk=chunk)
    return pl.pallas_call(
        kern,
        grid_spec=pltpu.PrefetchScalarGridSpec(
            num_scalar_prefetch=1,
            grid=(batch, heads, nc),
            in_specs=[
                pl.BlockSpec((chunk, dk), lambda b, h, n, lg: (b * nc + n, h)),
                pl.BlockSpec((chunk, dk), lambda b, h, n, lg: (b * nc + n, heads + h)),
                pl.BlockSpec((chunk, dv), lambda b, h, n, lg: (b * nc + n, v_off + h)),
                pl.BlockSpec((chunk, dv), lambda b, h, n, lg: (b * nc + n, g_off + h)),
            ],
            out_specs=pl.BlockSpec((chunk, dv), lambda b, h, n, lg: (b * nc + n, h)),
            scratch_shapes=[pltpu.VMEM((dk, dv), F32)],
        ),
        out_shape=jax.ShapeDtypeStruct((m, heads * dv), BF16),
        compiler_params=_cparams(("parallel", "parallel", "arbitrary")),
        name="retention",
    )(log_gamma, proj, proj, proj, proj)


def kernel(x, fox_w_in, fox_b_f, fox_q_gain, fox_k_gain, fox_w_out, ret_w_in, ret_w_out,
           ln_gain, ln_bias):
    batch, seq, d_model = x.shape
    depth = ln_gain.shape[0]
    m = batch * seq
    alpha = (2.0 * depth) ** 0.25

    fox_width = fox_w_out.shape[1]
    ret_v_width = ret_w_out.shape[1]
    ret_heads = ret_v_width // RET_V_DIM
    ret_qk_width = ret_heads * RET_QK_DIM

    half = RET_QK_DIM // 2
    inv_freq = ROPE_BASE ** (-jnp.arange(0, RET_QK_DIM, 2, dtype=F32) / RET_QK_DIM)
    ang = jnp.arange(seq, dtype=F32)[:, None] * inv_freq[None, :]
    cos, sin = jnp.cos(ang), jnp.sin(ang)
    log_gamma = jnp.log1p(-jnp.exp2(-5.0 - jnp.arange(ret_heads, dtype=F32)))

    h = x.reshape(m, d_model)
    h_bf = h.astype(BF16)
    for i in range(depth):
        j = i // 2
        if i % 2 == 0:
            w_in = fox_w_in[j]
            w_main = w_in[:, :4 * fox_width].astype(BF16)
            w_f = w_in[:, 4 * fox_width:].astype(BF16)
            proj = _fox_in_proj(h_bf, w_main, fox_q_gain[j], fox_k_gain[j])
            c = _fox_forget_cumsum(h_bf.reshape(batch, seq, d_model), w_f, fox_b_f[j])
            y = _fox_attention(proj, c, batch, seq)
            w_out = fox_w_out[j].astype(BF16)
        else:
            proj = _ret_in_proj(h_bf, ret_w_in[j].astype(BF16), cos, sin, seq,
                                ret_qk_width, ret_v_width)
            y = _retention(proj, log_gamma, batch, seq, ret_heads)
            w_out = ret_w_out[j].astype(BF16)
        h, h_bf = _out_proj_ln(y, w_out, h, ln_gain[i], ln_bias[i], alpha)
    return h.reshape(batch, seq, d_model)
```

```python
import functools
import math

import jax
import jax.numpy as jnp
from jax import lax
from jax.experimental import pallas as pl
from jax.experimental.pallas import tpu as pltpu

F32 = jnp.float32
BF16 = jnp.bfloat16

LANES = 128

FOX_HEAD_DIM = 128
RET_QK_DIM = 256
RET_V_DIM = 512
ROPE_BASE = 10000.0
LN_EPS = 1e-5
GN_EPS = 1e-6
QK_EPS = 1e-6
LOG2E = math.log2(math.e)

VMEM_LIMIT = 56 * 1024 * 1024


def _cparams(sem):
    return pltpu.CompilerParams(dimension_semantics=sem, vmem_limit_bytes=VMEM_LIMIT)


def _silu(g):
    return g * (1.0 / (1.0 + jnp.exp(-g)))


def _fox_in_kernel(x_ref, w_ref, qg_ref, kg_ref, o_ref, *, tn, width, q_scale):
    j = pl.program_id(1)
    acc = jnp.dot(x_ref[...], w_ref[...], preferred_element_type=F32)
    per_region = width // tn

    def rms(g_ref, scale):
        for hh in range(tn // FOX_HEAD_DIM):
            sl = slice(hh * FOX_HEAD_DIM, (hh + 1) * FOX_HEAD_DIM)
            a = acc[:, sl]
            ms = jnp.mean(a * a, axis=-1, keepdims=True)
            y = a * lax.rsqrt(ms + QK_EPS) * g_ref[...]
            if scale is not None:
                y = y * scale
            o_ref[:, sl] = y.astype(o_ref.dtype)

    @pl.when(j < per_region)
    def _():
        rms(qg_ref, q_scale)

    @pl.when((j >= per_region) & (j < 2 * per_region))
    def _():
        rms(kg_ref, None)

    @pl.when((j >= 2 * per_region) & (j < 3 * per_region))
    def _():
        o_ref[...] = acc.astype(o_ref.dtype)

    @pl.when(j >= 3 * per_region)
    def _():
        o_ref[...] = _silu(acc).astype(o_ref.dtype)


def _fox_in_proj(x_bf, w_bf, q_gain, k_gain, *, tm=1024, tn=512):
    m, d = x_bf.shape
    n = w_bf.shape[1]
    width = n // 4
    kern = functools.partial(_fox_in_kernel, tn=tn, width=width,
                             q_scale=FOX_HEAD_DIM ** -0.5 * LOG2E)
    return pl.pallas_call(
        kern,
        grid=(m // tm, n // tn),
        in_specs=[
            pl.BlockSpec((tm, d), lambda i, j: (i, 0)),
            pl.BlockSpec((d, tn), lambda i, j: (0, j)),
            pl.BlockSpec((1, FOX_HEAD_DIM), lambda i, j: (0, 0)),
            pl.BlockSpec((1, FOX_HEAD_DIM), lambda i, j: (0, 0)),
        ],
        out_specs=pl.BlockSpec((tm, tn), lambda i, j: (i, j)),
        out_shape=jax.ShapeDtypeStruct((m, n), BF16),
        compiler_params=_cparams(("parallel", "arbitrary")),
        name="fox_in_proj",
    )(x_bf, w_bf, q_gain.reshape(1, -1), k_gain.reshape(1, -1))


def _split3(v):
    hi = v.astype(BF16)
    r = v - hi.astype(F32)
    mid = r.astype(BF16)
    lo = (r - mid.astype(F32)).astype(BF16)
    return hi, mid, lo


def _fgate_kernel(x_ref, w_ref, b_ref, c_ref, carry_ref):
    ts = x_ref.shape[0]

    @pl.when(pl.program_id(1) == 0)
    def _():
        carry_ref[...] = jnp.zeros_like(carry_ref)

    z = jnp.dot(x_ref[...], w_ref[...], preferred_element_type=F32) + b_ref[...]
    log_f = jnp.minimum(z, 0.0) - jnp.log1p(jnp.exp(-jnp.abs(z)))
    row = lax.broadcasted_iota(jnp.int32, (ts, ts), 0)
    col = lax.broadcasted_iota(jnp.int32, (ts, ts), 1)
    tri = jnp.where(col <= row, 1.0, 0.0).astype(BF16)
    cs = carry_ref[...]
    for part in _split3(log_f):
        cs = cs + jnp.dot(tri, part, preferred_element_type=F32)
    c_ref[...] = cs
    carry_ref[...] = cs[ts - 1:ts, :]


def _fox_forget_cumsum(x_bf3, wf_bf, b_f, *, ts=512):
    b, s, d = x_bf3.shape
    h = wf_bf.shape[1]
    return pl.pallas_call(
        _fgate_kernel,
        grid=(b, s // ts),
        in_specs=[
            pl.BlockSpec((None, ts, d), lambda bi, si: (bi, si, 0)),
            pl.BlockSpec((d, h), lambda bi, si: (0, 0)),
            pl.BlockSpec((1, h), lambda bi, si: (0, 0)),
        ],
        out_specs=pl.BlockSpec((None, ts, h), lambda bi, si: (bi, si, 0)),
        out_shape=jax.ShapeDtypeStruct((b, s, h), F32),
        scratch_shapes=[pltpu.VMEM((1, h), F32)],
        compiler_params=_cparams(("parallel", "arbitrary")),
        name="fox_forget_cumsum",
    )(x_bf3, wf_bf, b_f.reshape(1, -1))


def _fox_attn_kernel(q_ref, k_ref, v_ref, g_ref, c_ref, o_ref,
                     kaug_ref, vaug_ref, qx_ref, qa_ref, m_ref, acc_ref, *, tq, tk, n_split):
    h = pl.program_id(1)
    i = pl.program_id(2)
    s_len = k_ref.shape[0]
    n_heads = c_ref.shape[1]
    dh = FOX_HEAD_DIM

    @pl.when(i == 0)
    def _():
        parts = _split3(c_ref[...] * LOG2E)
        row = lax.broadcasted_iota(jnp.int32, (n_heads, LANES), 0)
        col = lax.broadcasted_iota(jnp.int32, (n_heads, LANES), 1)
        xq = jnp.zeros((s_len, LANES), F32)
        xk = jnp.zeros((s_len, LANES), F32)
        for t, part in enumerate(parts):
            sel_q = jnp.where((row == h) & (col == t), 1.0, 0.0).astype(BF16)
            sel_k = jnp.where((row == h) & (col == 3 + t), -1.0, 0.0).astype(BF16)
            xq = xq + jnp.dot(part, sel_q, preferred_element_type=F32)
            xk = xk + jnp.dot(part, sel_k, preferred_element_type=F32)
        lane = lax.broadcasted_iota(jnp.int32, (s_len, LANES), 1)
        qx_ref[...] = (xq + jnp.where((lane >= 3) & (lane < 6), 1.0, 0.0)).astype(BF16)
        kaug_ref[:, :dh] = k_ref[...]
        kaug_ref[:, dh:] = (xk + jnp.where(lane < 3, 1.0, 0.0)).astype(BF16)
        vaug_ref[:, :dh] = v_ref[...]
        vaug_ref[:, dh:] = jnp.ones((s_len, dh), BF16)

    q0 = pl.multiple_of(i * tq, tq)
    qa_ref[:, :dh] = q_ref[...]
    qa_ref[:, dh:] = qx_ref[pl.ds(q0, tq), :]
    m_ref[...] = jnp.full(m_ref.shape, -jnp.inf, F32)
    acc_ref[...] = jnp.zeros_like(acc_ref)

    def block(r0, nr, k0, masked):
        rows = pl.ds(r0, nr)
        s = lax.dot_general(qa_ref[rows, :], kaug_ref[pl.ds(k0, tk), :],
                            (((1,), (1,)), ((), ())), preferred_element_type=F32)
        if masked:
            row = lax.broadcasted_iota(jnp.int32, (nr, tk), 0)
            col = lax.broadcasted_iota(jnp.int32, (nr, tk), 1)
            s = jnp.where(col <= row, s, -jnp.inf)
        m_old = m_ref[rows, :]
        m_new = jnp.maximum(m_old, jnp.max(s, axis=-1, keepdims=True))
        alpha = jnp.exp2(m_old - m_new)
        p = jnp.exp2(s - jnp.concatenate([m_new] * (tk // LANES), axis=1))
        pv = jnp.dot(p.astype(BF16), vaug_ref[pl.ds(k0, tk), :], preferred_element_type=F32)
        acc_ref[rows, :] = jnp.concatenate([alpha, alpha], axis=1) * acc_ref[rows, :] + pv
        m_ref[rows, :] = m_new

    per_step = tq // tk

    def full_blocks(j, carry):
        for u in range(per_step):
            k0 = pl.multiple_of((j * per_step + u) * tk, tk)
            for r0 in range(0, tq, tq // n_split):
                block(r0, tq // n_split, k0, False)
        return carry

    lax.fori_loop(0, i, full_blocks, 0)
    for t in range(tq // tk):
        k0 = pl.multiple_of(q0 + t * tk, tk)
        block(t * tk, tk, k0, True)
        if (t + 1) * tk < tq:
            block((t + 1) * tk, tq - (t + 1) * tk, k0, False)
    o_ref[...] = (acc_ref[:, :dh] / acc_ref[:, dh:] * g_ref[...].astype(F32)).astype(o_ref.dtype)


def _fox_attention(proj, c, batch, seq, *, tq=1024, tk=512, n_split=2):
    m, n = proj.shape
    width = n // 4
    heads = width // FOX_HEAD_DIM
    nq = seq // tq
    dh = FOX_HEAD_DIM
    kern = functools.partial(_fox_attn_kernel, tq=tq, tk=tk, n_split=n_split)
    return pl.pallas_call(
        kern,
        grid=(batch, heads, nq),
        in_specs=[
            pl.BlockSpec((tq, dh), lambda b, h, i: (b * nq + i, h)),
            pl.BlockSpec((seq, dh), lambda b, h, i: (b, heads + h)),
            pl.BlockSpec((seq, dh), lambda b, h, i: (b, 2 * heads + h)),
            pl.BlockSpec((tq, dh), lambda b, h, i: (b * nq + i, 3 * heads + h)),
            pl.BlockSpec((None, seq, heads), lambda b, h, i: (b, 0, 0)),
        ],
        out_specs=pl.BlockSpec((tq, dh), lambda b, h, i: (b * nq + i, h)),
        out_shape=jax.ShapeDtypeStruct((m, width), BF16),
        scratch_shapes=[
            pltpu.VMEM((seq, 2 * dh), BF16),
            pltpu.VMEM((seq, 2 * dh), BF16),
            pltpu.VMEM((seq, LANES), BF16),
            pltpu.VMEM((tq, 2 * dh), BF16),
            pltpu.VMEM((tq, LANES), F32),
            pltpu.VMEM((tq, 2 * dh), F32),
        ],
        compiler_params=_cparams(("parallel", "parallel", "arbitrary")),
        name="fox_attention",
    )(proj, proj, proj, proj, c)


def _out_ln_kernel(y_ref, w_ref, h_ref, g_ref, b_ref, o_ref, obf_ref, *, alpha):
    br = jnp.dot(y_ref[...], w_ref[...], preferred_element_type=F32)
    z = alpha * h_ref[...] + br
    mu = jnp.mean(z, axis=-1, keepdims=True)
    zc = z - mu
    var = jnp.mean(zc * zc, axis=-1, keepdims=True)
    out = zc * lax.rsqrt(var + LN_EPS) * g_ref[...] + b_ref[...]
    o_ref[...] = out
    obf_ref[...] = out.astype(obf_ref.dtype)


def _out_proj_ln(y_bf, w_bf, h, gain, bias, alpha, *, tm=256):
    m, k = y_bf.shape
    d = w_bf.shape[1]
    kern = functools.partial(_out_ln_kernel, alpha=alpha)
    return pl.pallas_call(
        kern,
        grid=(m // tm,),
        in_specs=[
            pl.BlockSpec((tm, k), lambda i: (i, 0)),
            pl.BlockSpec((k, d), lambda i: (0, 0)),
            pl.BlockSpec((tm, d), lambda i: (i, 0)),
            pl.BlockSpec((1, d), lambda i: (0, 0)),
            pl.BlockSpec((1, d), lambda i: (0, 0)),
        ],
        out_specs=[pl.BlockSpec((tm, d), lambda i: (i, 0)),
                   pl.BlockSpec((tm, d), lambda i: (i, 0))],
        out_shape=[jax.ShapeDtypeStruct((m, d), F32), jax.ShapeDtypeStruct((m, d), BF16)],
        compiler_params=_cparams(("parallel",)),
        name="out_proj_ln",
    )(y_bf, w_bf, h, gain.reshape(1, -1), bias.reshape(1, -1))


def _ret_in_kernel(x_ref, w_ref, cos_ref, sin_ref, o_ref, *, tn, qk_width, v_width, k_scale):
    j = pl.program_id(1)
    acc = jnp.dot(x_ref[...], w_ref[...], preferred_element_type=F32)
    half = RET_QK_DIM // 2
    nq = qk_width // tn
    nv = v_width // tn

    def rope(scale):
        cos = cos_ref[...]
        sin = sin_ref[...]
        for hh in range(tn // RET_QK_DIM):
            base = hh * RET_QK_DIM
            t1 = acc[:, base:base + half]
            t2 = acc[:, base + half:base + RET_QK_DIM]
            r1 = t1 * cos - t2 * sin
            r2 = t1 * sin + t2 * cos
            if scale is not None:
                r1 = r1 * scale
                r2 = r2 * scale
            o_ref[:, base:base + half] = r1.astype(o_ref.dtype)
            o_ref[:, base + half:base + RET_QK_DIM] = r2.astype(o_ref.dtype)

    @pl.when(j < nq)
    def _():
        rope(None)

    @pl.when((j >= nq) & (j < 2 * nq))
    def _():
        rope(k_scale)

    @pl.when((j >= 2 * nq) & (j < 2 * nq + nv))
    def _():
        o_ref[...] = acc.astype(o_ref.dtype)

    @pl.when(j >= 2 * nq + nv)
    def _():
        o_ref[...] = _silu(acc).astype(o_ref.dtype)


def _ret_in_proj(x_bf, w_bf, cos, sin, seq, qk_width, v_width, *, tm=1024, tn=512):
    m, d = x_bf.shape
    n = w_bf.shape[1]
    half = RET_QK_DIM // 2
    pos_blocks = seq // tm
    kern = functools.partial(_ret_in_kernel, tn=tn, qk_width=qk_width, v_width=v_width,
                             k_scale=RET_QK_DIM ** -0.5)
    return pl.pallas_call(
        kern,
        grid=(m // tm, n // tn),
        in_specs=[
            pl.BlockSpec((tm, d), lambda i, j: (i, 0)),
            pl.BlockSpec((d, tn), lambda i, j: (0, j)),
            pl.BlockSpec((tm, half), lambda i, j: (i % pos_blocks, 0)),
            pl.BlockSpec((tm, half), lambda i, j: (i % pos_blocks, 0)),
        ],
        out_specs=pl.BlockSpec((tm, tn), lambda i, j: (i, j)),
        out_shape=jax.ShapeDtypeStruct((m, n), BF16),
        compiler_params=_cparams(("parallel", "arbitrary")),
        name="ret_in_proj",
    )(x_bf, w_bf, cos, sin)


def _retention_kernel(lg_ref, q_ref, k_ref, v_ref, g_ref, o_ref, r_ref, *, chunk):
    h = pl.program_id(1)
    n = pl.program_id(2)

    @pl.when(n == 0)
    def _():
        r_ref[...] = jnp.zeros_like(r_ref)

    lg = lg_ref[h]
    row = lax.broadcasted_iota(jnp.int32, (chunk, chunk), 0)
    col = lax.broadcasted_iota(jnp.int32, (chunk, chunk), 1)
    diff = (row - col).astype(F32)
    intra = jnp.where(diff >= 0, jnp.exp(jnp.maximum(diff, 0.0) * lg), 0.0)
    pos = lax.broadcasted_iota(jnp.int32, (chunk, 1), 0).astype(F32)
    q_decay = jnp.exp((pos + 1.0) * lg)
    k_decay = jnp.exp((chunk - 1.0 - pos) * lg)
    c_decay = jnp.exp(jnp.full((1, 1), chunk, F32) * lg)

    q = q_ref[...]
    k = k_ref[...]
    v = v_ref[...]
    r = r_ref[...]
    inner = lax.dot_general(q, k, (((1,), (1,)), ((), ())), preferred_element_type=F32) * intra
    o = (jnp.dot(inner.astype(BF16), v, preferred_element_type=F32)
         + jnp.dot(q, r.astype(BF16), preferred_element_type=F32) * q_decay)
    kd = (k.astype(F32) * k_decay).astype(BF16)
    r_ref[...] = r * c_decay + lax.dot_general(kd, v, (((0,), (0,)), ((), ())),
                                               preferred_element_type=F32)

    mu = jnp.mean(o, axis=-1, keepdims=True)
    oc = o - mu
    var = jnp.mean(oc * oc, axis=-1, keepdims=True)
    o_ref[...] = (oc * lax.rsqrt(var + GN_EPS) * g_ref[...].astype(F32)).astype(o_ref.dtype)


def _retention(proj, log_gamma, batch, seq, heads, *, chunk=256):
    m = proj.shape[0]
    nc = seq // chunk
    dk, dv = RET_QK_DIM, RET_V_DIM
    v_off = 2 * heads * dk // dv
    g_off = v_off + heads
    kern = functools.partial(_retention_kernel, chunk=chunk)
    return pl.pallas_call(
        kern,
        grid_spec=pltpu.PrefetchScalarGridSpec(
            num_scalar_prefetch=1,
            grid=(batch, heads, nc),
            in_specs=[
                pl.BlockSpec((chunk, dk), lambda b, h, n, lg: (b * nc + n, h)),
                pl.BlockSpec((chunk, dk), lambda b, h, n, lg: (b * nc + n, heads + h)),
                pl.BlockSpec((chunk, dv), lambda b, h, n, lg: (b * nc + n, v_off + h)),
                pl.BlockSpec((chunk, dv), lambda b, h, n, lg: (b * nc + n, g_off + h)),
            ],
            out_specs=pl.BlockSpec((chunk, dv), lambda b, h, n, lg: (b * nc + n, h)),
            scratch_shapes=[pltpu.VMEM((dk, dv), F32)],
        ),
        out_shape=jax.ShapeDtypeStruct((m, heads * dv), BF16),
        compiler_params=_cparams(("parallel", "parallel", "arbitrary")),
        name="retention",
    )(log_gamma, proj, proj, proj, proj)


def kernel(x, fox_w_in, fox_b_f, fox_q_gain, fox_k_gain, fox_w_out, ret_w_in, ret_w_out,
           ln_gain, ln_bias):
    batch, seq, d_model = x.shape
    depth = ln_gain.shape[0]
    m = batch * seq
    alpha = (2.0 * depth) ** 0.25

    fox_width = fox_w_out.shape[1]
    ret_v_width = ret_w_out.shape[1]
    ret_heads = ret_v_width // RET_V_DIM
    ret_qk_width = ret_heads * RET_QK_DIM

    half = RET_QK_DIM // 2
    inv_freq = ROPE_BASE ** (-jnp.arange(0, RET_QK_DIM, 2, dtype=F32) / RET_QK_DIM)
    ang = jnp.arange(seq, dtype=F32)[:, None] * inv_freq[None, :]
    cos, sin = jnp.cos(ang), jnp.sin(ang)
    log_gamma = jnp.log1p(-jnp.exp2(-5.0 - jnp.arange(ret_heads, dtype=F32)))

    h = x.reshape(m, d_model)
    h_bf = h.astype(BF16)
    for i in range(depth):
        j = i // 2
        if i % 2 == 0:
            w_in = fox_w_in[j]
            w_main = w_in[:, :4 * fox_width].astype(BF16)
            w_f = w_in[:, 4 * fox_width:].astype(BF16)
            proj = _fox_in_proj(h_bf, w_main, fox_q_gain[j], fox_k_gain[j])
            c = _fox_forget_cumsum(h_bf.reshape(batch, seq, d_model), w_f, fox_b_f[j])
            y = _fox_attention(proj, c, batch, seq)
            w_out = fox_w_out[j].astype(BF16)
        else:
            proj = _ret_in_proj(h_bf, ret_w_in[j].astype(BF16), cos, sin, seq,
                                ret_qk_width, ret_v_width)
            y = _retention(proj, log_gamma, batch, seq, ret_heads)
            w_out = ret_w_out[j].astype(BF16)
        h, h_bf = _out_proj_ln(y, w_out, h, ln_gain[i], ln_bias[i], alpha)
    return h.reshape(batch, seq, d_model)
```

```python
import functools
import math

import jax
import jax.numpy as jnp
from jax import lax
from jax.experimental import pallas as pl
from jax.experimental.pallas import tpu as pltpu

F32 = jnp.float32
BF16 = jnp.bfloat16

LANES = 128
MXU_COLS = 256

FOX_HEAD_DIM = 128
RET_QK_DIM = 256
RET_V_DIM = 512
ROPE_BASE = 10000.0
LN_EPS = 1e-5
GN_EPS = 1e-6
QK_EPS = 1e-6
LOG2E = math.log2(math.e)

VMEM_LIMIT = 56 * 1024 * 1024

NT_DIMS = (((1,), (1,)), ((), ()))
TN_DIMS = (((0,), (0,)), ((), ()))


def _cparams(sem):
    return pltpu.CompilerParams(dimension_semantics=sem, vmem_limit_bytes=VMEM_LIMIT)


def _silu(g):
    return g * (1.0 / (1.0 + jnp.exp(-g)))


def _dot_by_column_groups(x_ref, w_ref, o_ref, epilogue):
    tn = o_ref.shape[1]
    for c in range(tn // MXU_COLS):
        cols = slice(c * MXU_COLS, (c + 1) * MXU_COLS)
        acc = jnp.dot(x_ref[...], w_ref[:, cols].astype(BF16), preferred_element_type=F32)
        o_ref[:, cols] = epilogue(acc).astype(o_ref.dtype)


def _fox_in_kernel(x_ref, w_ref, qg_ref, kg_ref, o_ref, *, width, q_scale):
    j = pl.program_id(1)
    per_region = width // o_ref.shape[1]

    def rms(g_ref, scale):
        def epilogue(acc):
            outs = []
            for hh in range(MXU_COLS // FOX_HEAD_DIM):
                a = acc[:, hh * FOX_HEAD_DIM:(hh + 1) * FOX_HEAD_DIM]
                ms = jnp.mean(a * a, axis=-1, keepdims=True)
                y = a * lax.rsqrt(ms + QK_EPS) * g_ref[...]
                outs.append(y if scale is None else y * scale)
            return jnp.concatenate(outs, axis=1)
        return epilogue

    @pl.when(j < per_region)
    def _():
        _dot_by_column_groups(x_ref, w_ref, o_ref, rms(qg_ref, q_scale))

    @pl.when((j >= per_region) & (j < 2 * per_region))
    def _():
        _dot_by_column_groups(x_ref, w_ref, o_ref, rms(kg_ref, None))

    @pl.when((j >= 2 * per_region) & (j < 3 * per_region))
    def _():
        _dot_by_column_groups(x_ref, w_ref, o_ref, lambda acc: acc)

    @pl.when(j >= 3 * per_region)
    def _():
        _dot_by_column_groups(x_ref, w_ref, o_ref, _silu)


def _fox_in_proj(x_bf, w_in, layer, q_gain, k_gain, width, *, tm=2048, tn=512):
    m, d = x_bf.shape
    n = 4 * width
    kern = functools.partial(_fox_in_kernel, width=width, q_scale=FOX_HEAD_DIM ** -0.5 * LOG2E)
    return pl.pallas_call(
        kern,
        grid=(m // tm, n // tn),
        in_specs=[
            pl.BlockSpec((tm, d), lambda i, j: (i, 0)),
            pl.BlockSpec((None, d, tn), lambda i, j: (layer, 0, j)),
            pl.BlockSpec((1, FOX_HEAD_DIM), lambda i, j: (0, 0)),
            pl.BlockSpec((1, FOX_HEAD_DIM), lambda i, j: (0, 0)),
        ],
        out_specs=pl.BlockSpec((tm, tn), lambda i, j: (i, j)),
        out_shape=jax.ShapeDtypeStruct((m, n), BF16),
        compiler_params=_cparams(("parallel", "arbitrary")),
        name="fox_in_proj",
    )(x_bf, w_in, q_gain.reshape(1, -1), k_gain.reshape(1, -1))


def _ret_in_kernel(x_ref, w_ref, cos_ref, sin_ref, o_ref, *, qk_width, v_width, k_scale):
    j = pl.program_id(1)
    tn = o_ref.shape[1]
    half = RET_QK_DIM // 2
    nq = qk_width // tn
    nv = v_width // tn

    def rope(scale):
        def epilogue(acc):
            cos = cos_ref[...]
            sin = sin_ref[...]
            t1 = acc[:, :half]
            t2 = acc[:, half:]
            r = jnp.concatenate([t1 * cos - t2 * sin, t1 * sin + t2 * cos], axis=1)
            return r if scale is None else r * scale
        return epilogue

    @pl.when(j < nq)
    def _():
        _dot_by_column_groups(x_ref, w_ref, o_ref, rope(None))

    @pl.when((j >= nq) & (j < 2 * nq))
    def _():
        _dot_by_column_groups(x_ref, w_ref, o_ref, rope(k_scale))

    @pl.when((j >= 2 * nq) & (j < 2 * nq + nv))
    def _():
        _dot_by_column_groups(x_ref, w_ref, o_ref, lambda acc: acc)

    @pl.when(j >= 2 * nq + nv)
    def _():
        _dot_by_column_groups(x_ref, w_ref, o_ref, _silu)


def _ret_in_proj(x_bf, w_in, layer, cos, sin, seq, qk_width, v_width, *, tm=2048, tn=512):
    assert RET_QK_DIM == MXU_COLS
    m, d = x_bf.shape
    n = w_in.shape[2]
    half = RET_QK_DIM // 2
    pos_blocks = seq // tm
    kern = functools.partial(_ret_in_kernel, qk_width=qk_width, v_width=v_width,
                             k_scale=RET_QK_DIM ** -0.5)
    return pl.pallas_call(
        kern,
        grid=(m // tm, n // tn),
        in_specs=[
            pl.BlockSpec((tm, d), lambda i, j: (i, 0)),
            pl.BlockSpec((None, d, tn), lambda i, j: (layer, 0, j)),
            pl.BlockSpec((tm, half), lambda i, j: (i % pos_blocks, 0)),
            pl.BlockSpec((tm, half), lambda i, j: (i % pos_blocks, 0)),
        ],
        out_specs=pl.BlockSpec((tm, tn), lambda i, j: (i, j)),
        out_shape=jax.ShapeDtypeStruct((m, n), BF16),
        compiler_params=_cparams(("parallel", "arbitrary")),
        name="ret_in_proj",
    )(x_bf, w_in, cos, sin)


def _split3(v):
    hi = v.astype(BF16)
    r = v - hi.astype(F32)
    mid = r.astype(BF16)
    lo = (r - mid.astype(F32)).astype(BF16)
    return hi, mid, lo


def _fgate_kernel(x_ref, w_ref, b_ref, c_ref, carry_ref):
    ts = x_ref.shape[0]

    @pl.when(pl.program_id(1) == 0)
    def _():
        carry_ref[...] = jnp.zeros_like(carry_ref)

    z = jnp.dot(x_ref[...], w_ref[...], preferred_element_type=F32) + b_ref[...]
    log_f = jnp.minimum(z, 0.0) - jnp.log1p(jnp.exp(-jnp.abs(z)))
    row = lax.broadcasted_iota(jnp.int32, (ts, ts), 0)
    col = lax.broadcasted_iota(jnp.int32, (ts, ts), 1)
    tri = jnp.where(col <= row, 1.0, 0.0).astype(BF16)
    cs = carry_ref[...]
    for part in _split3(log_f):
        cs = cs + jnp.dot(tri, part, preferred_element_type=F32)
    c_ref[...] = cs
    carry_ref[...] = cs[ts - 1:ts, :]


def _fox_forget_cumsum(x_bf3, wf_bf, b_f, *, ts=512):
    b, s, d = x_bf3.shape
    h = wf_bf.shape[1]
    return pl.pallas_call(
        _fgate_kernel,
        grid=(b, s // ts),
        in_specs=[
            pl.BlockSpec((None, ts, d), lambda bi, si: (bi, si, 0)),
            pl.BlockSpec((d, h), lambda bi, si: (0, 0)),
            pl.BlockSpec((1, h), lambda bi, si: (0, 0)),
        ],
        out_specs=pl.BlockSpec((None, ts, h), lambda bi, si: (bi, si, 0)),
        out_shape=jax.ShapeDtypeStruct((b, s, h), F32),
        scratch_shapes=[pltpu.VMEM((1, h), F32)],
        compiler_params=_cparams(("parallel", "arbitrary")),
        name="fox_forget_cumsum",
    )(x_bf3, wf_bf, b_f.reshape(1, -1))


def _fox_attn_kernel(q_ref, k_ref, v_ref, g_ref, c_ref, o_ref,
                     kaug_ref, vaug_ref, qx_ref, qa_ref, m_ref, acc_ref, *, tq, tk, n_split):
    h = pl.program_id(1)
    i = pl.program_id(2)
    s_len = k_ref.shape[0]
    n_heads = c_ref.shape[1]
    dh = FOX_HEAD_DIM

    @pl.when(i == 0)
    def _():
        parts = _split3(c_ref[...] * LOG2E)
        row = lax.broadcasted_iota(jnp.int32, (n_heads, LANES), 0)
        col = lax.broadcasted_iota(jnp.int32, (n_heads, LANES), 1)
        xq = jnp.zeros((s_len, LANES), F32)
        xk = jnp.zeros((s_len, LANES), F32)
        for t, part in enumerate(parts):
            sel_q = jnp.where((row == h) & (col == t), 1.0, 0.0).astype(BF16)
            sel_k = jnp.where((row == h) & (col == 3 + t), -1.0, 0.0).astype(BF16)
            xq = xq + jnp.dot(part, sel_q, preferred_element_type=F32)
            xk = xk + jnp.dot(part, sel_k, preferred_element_type=F32)
        lane = lax.broadcasted_iota(jnp.int32, (s_len, LANES), 1)
        qx_ref[...] = (xq + jnp.where((lane >= 3) & (lane < 6), 1.0, 0.0)).astype(BF16)
        kaug_ref[:, :dh] = k_ref[...]
        kaug_ref[:, dh:] = (xk + jnp.where(lane < 3, 1.0, 0.0)).astype(BF16)
        vaug_ref[:, :dh] = v_ref[...]
        vaug_ref[:, dh:] = jnp.ones((s_len, dh), BF16)

    q0 = pl.multiple_of(i * tq, tq)
    qa_ref[:, :dh] = q_ref[...]
    qa_ref[:, dh:] = qx_ref[pl.ds(q0, tq), :]
    m_ref[...] = jnp.full(m_ref.shape, -jnp.inf, F32)
    acc_ref[...] = jnp.zeros_like(acc_ref)

    def block(r0, nr, k0, masked):
        rows = pl.ds(r0, nr)
        s = lax.dot_general(qa_ref[rows, :], kaug_ref[pl.ds(k0, tk), :], NT_DIMS,
                            preferred_element_type=F32)
        if masked:
            row = lax.broadcasted_iota(jnp.int32, (nr, tk), 0)
            col = lax.broadcasted_iota(jnp.int32, (nr, tk), 1)
            s = jnp.where(col <= row, s, -jnp.inf)
        m_old = m_ref[rows, :]
        m_new = jnp.maximum(m_old, jnp.max(s, axis=-1, keepdims=True))
        alpha = jnp.exp2(m_old - m_new)
        p = jnp.exp2(s - jnp.concatenate([m_new] * (tk // LANES), axis=1))
        pv = jnp.dot(p.astype(BF16), vaug_ref[pl.ds(k0, tk), :], preferred_element_type=F32)
        acc_ref[rows, :] = jnp.concatenate([alpha, alpha], axis=1) * acc_ref[rows, :] + pv
        m_ref[rows, :] = m_new

    per_step = tq // tk

    def full_blocks(j, carry):
        for u in range(per_step):
            k0 = pl.multiple_of((j * per_step + u) * tk, tk)
            for r0 in range(0, tq, tq // n_split):
                block(r0, tq // n_split, k0, False)
        return carry

    lax.fori_loop(0, i, full_blocks, 0)
    for t in range(tq // tk):
        k0 = pl.multiple_of(q0 + t * tk, tk)
        block(t * tk, tk, k0, True)
        if (t + 1) * tk < tq:
            block((t + 1) * tk, tq - (t + 1) * tk, k0, False)
    o_ref[...] = (acc_ref[:, :dh] / acc_ref[:, dh:] * g_ref[...].astype(F32)).astype(o_ref.dtype)


def _fox_attention(proj, c, batch, seq, *, tq=1024, tk=512, n_split=2):
    m, n = proj.shape
    width = n // 4
    heads = width // FOX_HEAD_DIM
    nq = seq // tq
    dh = FOX_HEAD_DIM
    kern = functools.partial(_fox_attn_kernel, tq=tq, tk=tk, n_split=n_split)
    return pl.pallas_call(
        kern,
        grid=(batch, heads, nq),
        in_specs=[
            pl.BlockSpec((tq, dh), lambda b, h, i: (b * nq + i, h)),
            pl.BlockSpec((seq, dh), lambda b, h, i: (b, heads + h)),
            pl.BlockSpec((seq, dh), lambda b, h, i: (b, 2 * heads + h)),
            pl.BlockSpec((tq, dh), lambda b, h, i: (b * nq + i, 3 * heads + h)),
            pl.BlockSpec((None, seq, heads), lambda b, h, i: (b, 0, 0)),
        ],
        out_specs=pl.BlockSpec((tq, dh), lambda b, h, i: (b * nq + i, h)),
        out_shape=jax.ShapeDtypeStruct((m, width), BF16),
        scratch_shapes=[
            pltpu.VMEM((seq, 2 * dh), BF16),
            pltpu.VMEM((seq, 2 * dh), BF16),
            pltpu.VMEM((seq, LANES), BF16),
            pltpu.VMEM((tq, 2 * dh), BF16),
            pltpu.VMEM((tq, LANES), F32),
            pltpu.VMEM((tq, 2 * dh), F32),
        ],
        compiler_params=_cparams(("parallel", "parallel", "arbitrary")),
        name="fox_attention",
    )(proj, proj, proj, proj, c)


def _out_ln_kernel(y_ref, w_ref, h_ref, g_ref, b_ref, o_ref, obf_ref, *, alpha):
    br = jnp.dot(y_ref[...], w_ref[...], preferred_element_type=F32)
    z = alpha * h_ref[...] + br
    mu = jnp.mean(z, axis=-1, keepdims=True)
    zc = z - mu
    var = jnp.mean(zc * zc, axis=-1, keepdims=True)
    out = zc * lax.rsqrt(var + LN_EPS) * g_ref[...] + b_ref[...]
    o_ref[...] = out
    obf_ref[...] = out.astype(obf_ref.dtype)


def _out_proj_ln(y_bf, w_bf, h, gain, bias, alpha, *, tm=256):
    m, k = y_bf.shape
    d = w_bf.shape[1]
    kern = functools.partial(_out_ln_kernel, alpha=alpha)
    return pl.pallas_call(
        kern,
        grid=(m // tm,),
        in_specs=[
            pl.BlockSpec((tm, k), lambda i: (i, 0)),
            pl.BlockSpec((k, d), lambda i: (0, 0)),
            pl.BlockSpec((tm, d), lambda i: (i, 0)),
            pl.BlockSpec((1, d), lambda i: (0, 0)),
            pl.BlockSpec((1, d), lambda i: (0, 0)),
        ],
        out_specs=[pl.BlockSpec((tm, d), lambda i: (i, 0)),
                   pl.BlockSpec((tm, d), lambda i: (i, 0))],
        out_shape=[jax.ShapeDtypeStruct((m, d), F32), jax.ShapeDtypeStruct((m, d), BF16)],
        compiler_params=_cparams(("parallel",)),
        name="out_proj_ln",
    )(y_bf, w_bf, h, gain.reshape(1, -1), bias.reshape(1, -1))


def _retention_kernel(lg_ref, q_ref, k_ref, v_ref, g_ref, o_ref, r_ref, *, chunk):
    h = pl.program_id(1)
    n = pl.program_id(2)

    @pl.when(n == 0)
    def _():
        r_ref[...] = jnp.zeros_like(r_ref)

    lg = lg_ref[h]
    row = lax.broadcasted_iota(jnp.int32, (chunk, chunk), 0)
    col = lax.broadcasted_iota(jnp.int32, (chunk, chunk), 1)
    diff = (row - col).astype(F32)
    intra = jnp.where(diff >= 0, jnp.exp(jnp.maximum(diff, 0.0) * lg), 0.0)
    pos = lax.broadcasted_iota(jnp.int32, (chunk, 1), 0).astype(F32)
    q_decay = jnp.exp((pos + 1.0) * lg)
    k_decay = jnp.exp((chunk - 1.0 - pos) * lg)
    c_decay = jnp.exp(jnp.full((1, 1), chunk, F32) * lg)

    r = r_ref[...]
    for ci in range(q_ref.shape[0] // chunk):
        rows = slice(ci * chunk, (ci + 1) * chunk)
        q = q_ref[rows, :]
        k = k_ref[rows, :]
        v = v_ref[rows, :]
        inner = lax.dot_general(q, k, NT_DIMS, preferred_element_type=F32) * intra
        o = (jnp.dot(inner.astype(BF16), v, preferred_element_type=F32)
             + jnp.dot(q, r.astype(BF16), preferred_element_type=F32) * q_decay)
        kd = (k.astype(F32) * k_decay).astype(BF16)
        r = r * c_decay + lax.dot_general(kd, v, TN_DIMS, preferred_element_type=F32)

        mu = jnp.mean(o, axis=-1, keepdims=True)
        oc = o - mu
        var = jnp.mean(oc * oc, axis=-1, keepdims=True)
        o_ref[rows, :] = (oc * lax.rsqrt(var + GN_EPS)
                          * g_ref[rows, :].astype(F32)).astype(o_ref.dtype)
    r_ref[...] = r


def _retention(proj, log_gamma, batch, seq, heads, *, chunk=256, chunks_per_step=4):
    m = proj.shape[0]
    ts = chunk * chunks_per_step
    ns = seq // ts
    dk, dv = RET_QK_DIM, RET_V_DIM
    v_off = 2 * heads * dk // dv
    g_off = v_off + heads
    kern = functools.partial(_retention_kernel, chunk=chunk)
    return pl.pallas_call(
        kern,
        grid_spec=pltpu.PrefetchScalarGridSpec(
            num_scalar_prefetch=1,
            grid=(batch, heads, ns),
            in_specs=[
                pl.BlockSpec((ts, dk), lambda b, h, n, lg: (b * ns + n, h)),
                pl.BlockSpec((ts, dk), lambda b, h, n, lg: (b * ns + n, heads + h)),
                pl.BlockSpec((ts, dv), lambda b, h, n, lg: (b * ns + n, v_off + h)),
                pl.BlockSpec((ts, dv), lambda b, h, n, lg: (b * ns + n, g_off + h)),
            ],
            out_specs=pl.BlockSpec((ts, dv), lambda b, h, n, lg: (b * ns + n, h)),
            scratch_shapes=[pltpu.VMEM((dk, dv), F32)],
        ),
        out_shape=jax.ShapeDtypeStruct((m, heads * dv), BF16),
        compiler_params=_cparams(("parallel", "parallel", "arbitrary")),
        name="retention",
    )(log_gamma, proj, proj, proj, proj)


def kernel(x, fox_w_in, fox_b_f, fox_q_gain, fox_k_gain, fox_w_out, ret_w_in, ret_w_out,
           ln_gain, ln_bias):
    batch, seq, d_model = x.shape
    depth = ln_gain.shape[0]
    m = batch * seq
    alpha = (2.0 * depth) ** 0.25

    fox_width = fox_w_out.shape[1]
    ret_v_width = ret_w_out.shape[1]
    ret_heads = ret_v_width // RET_V_DIM
    ret_qk_width = ret_heads * RET_QK_DIM

    inv_freq = ROPE_BASE ** (-jnp.arange(0, RET_QK_DIM, 2, dtype=F32) / RET_QK_DIM)
    ang = jnp.arange(seq, dtype=F32)[:, None] * inv_freq[None, :]
    cos, sin = jnp.cos(ang), jnp.sin(ang)
    log_gamma = jnp.log1p(-jnp.exp2(-5.0 - jnp.arange(ret_heads, dtype=F32)))

    h = x.reshape(m, d_model)
    h_bf = h.astype(BF16)
    for i in range(depth):
        j = i // 2
        if i % 2 == 0:
            proj = _fox_in_proj(h_bf, fox_w_in, j, fox_q_gain[j], fox_k_gain[j], fox_width)
            w_f = fox_w_in[j, :, 4 * fox_width:].astype(BF16)
            c = _fox_forget_cumsum(h_bf.reshape(batch, seq, d_model), w_f, fox_b_f[j])
            y = _fox_attention(proj, c, batch, seq)
            w_out = fox_w_out[j].astype(BF16)
        else:
            proj = _ret_in_proj(h_bf, ret_w_in, j, cos, sin, seq, ret_qk_width, ret_v_width)
            y = _retention(proj, log_gamma, batch, seq, ret_heads)
            w_out = ret_w_out[j].astype(BF16)
        h, h_bf = _out_proj_ln(y, w_out, h, ln_gain[i], ln_bias[i], alpha)
    return h.reshape(batch, seq, d_model)
```

```python
import functools
import math

import jax
import jax.numpy as jnp
from jax import lax
from jax.experimental import pallas as pl
from jax.experimental.pallas import tpu as pltpu

F32 = jnp.float32
BF16 = jnp.bfloat16

LANES = 128
MXU_COLS = 256

FOX_HEAD_DIM = 128
RET_QK_DIM = 256
RET_V_DIM = 512
ROPE_BASE = 10000.0
LN_EPS = 1e-5
GN_EPS = 1e-6
QK_EPS = 1e-6
LOG2E = math.log2(math.e)

VMEM_LIMIT = 56 * 1024 * 1024

NT_DIMS = (((1,), (1,)), ((), ()))
TN_DIMS = (((0,), (0,)), ((), ()))


def _cparams(sem):
    return pltpu.CompilerParams(dimension_semantics=sem, vmem_limit_bytes=VMEM_LIMIT)


def _silu(g):
    return g * (1.0 / (1.0 + jnp.exp(-g)))


def _dot_by_column_groups(x_ref, w_ref, o_ref, epilogue, w_transposed=False):
    tn = o_ref.shape[1]
    for c in range(tn // MXU_COLS):
        cols = slice(c * MXU_COLS, (c + 1) * MXU_COLS)
        if w_transposed:
            acc = lax.dot_general(x_ref[...], w_ref[cols, :].astype(BF16), NT_DIMS,
                                  preferred_element_type=F32)
        else:
            acc = jnp.dot(x_ref[...], w_ref[:, cols].astype(BF16), preferred_element_type=F32)
        o_ref[:, cols] = epilogue(acc).astype(o_ref.dtype)


def _fox_in_kernel(x_ref, w_ref, qg_ref, kg_ref, o_ref, *, width, q_scale):
    j = pl.program_id(1)
    per_region = width // o_ref.shape[1]

    def rms(g_ref, scale):
        def epilogue(acc):
            outs = []
            for hh in range(MXU_COLS // FOX_HEAD_DIM):
                a = acc[:, hh * FOX_HEAD_DIM:(hh + 1) * FOX_HEAD_DIM]
                ms = jnp.mean(a * a, axis=-1, keepdims=True)
                y = a * lax.rsqrt(ms + QK_EPS) * g_ref[...]
                outs.append(y if scale is None else y * scale)
            return jnp.concatenate(outs, axis=1)
        return epilogue

    @pl.when(j < per_region)
    def _():
        _dot_by_column_groups(x_ref, w_ref, o_ref, rms(qg_ref, q_scale), True)

    @pl.when((j >= per_region) & (j < 2 * per_region))
    def _():
        _dot_by_column_groups(x_ref, w_ref, o_ref, rms(kg_ref, None), True)

    @pl.when((j >= 2 * per_region) & (j < 3 * per_region))
    def _():
        _dot_by_column_groups(x_ref, w_ref, o_ref, lambda acc: acc, True)

    @pl.when(j >= 3 * per_region)
    def _():
        _dot_by_column_groups(x_ref, w_ref, o_ref, _silu, True)


def _fox_in_proj(x_bf, w_in_t, layer, q_gain, k_gain, width, *, tm=2048, tn=1024):
    m, d = x_bf.shape
    n = 4 * width
    kern = functools.partial(_fox_in_kernel, width=width, q_scale=FOX_HEAD_DIM ** -0.5 * LOG2E)
    return pl.pallas_call(
        kern,
        grid=(m // tm, n // tn),
        in_specs=[
            pl.BlockSpec((tm, d), lambda i, j: (i, 0), pipeline_mode=pl.Buffered(1)),
            pl.BlockSpec((None, tn, d), lambda i, j: (layer, j, 0)),
            pl.BlockSpec((1, FOX_HEAD_DIM), lambda i, j: (0, 0)),
            pl.BlockSpec((1, FOX_HEAD_DIM), lambda i, j: (0, 0)),
        ],
        out_specs=pl.BlockSpec((tm, tn), lambda i, j: (i, j)),
        out_shape=jax.ShapeDtypeStruct((m, n), BF16),
        compiler_params=_cparams(("parallel", "arbitrary")),
        name="fox_in_proj",
    )(x_bf, w_in_t, q_gain.reshape(1, -1), k_gain.reshape(1, -1))


def _ret_in_kernel(x_ref, w_ref, cos_ref, sin_ref, o_ref, *, qk_width, v_width, k_scale):
    j = pl.program_id(1)
    tn = o_ref.shape[1]
    half = RET_QK_DIM // 2
    nq = qk_width // tn
    nv = v_width // tn

    def rope(scale):
        def epilogue(acc):
            cos = cos_ref[...]
            sin = sin_ref[...]
            t1 = acc[:, :half]
            t2 = acc[:, half:]
            r = jnp.concatenate([t1 * cos - t2 * sin, t1 * sin + t2 * cos], axis=1)
            return r if scale is None else r * scale
        return epilogue

    @pl.when(j < nq)
    def _():
        _dot_by_column_groups(x_ref, w_ref, o_ref, rope(None))

    @pl.when((j >= nq) & (j < 2 * nq))
    def _():
        _dot_by_column_groups(x_ref, w_ref, o_ref, rope(k_scale))

    @pl.when((j >= 2 * nq) & (j < 2 * nq + nv))
    def _():
        _dot_by_column_groups(x_ref, w_ref, o_ref, lambda acc: acc)

    @pl.when(j >= 2 * nq + nv)
    def _():
        _dot_by_column_groups(x_ref, w_ref, o_ref, _silu)


def _ret_in_proj(x_bf, w_in, layer, cos, sin, seq, qk_width, v_width, *, tm=2048, tn=1024):
    assert RET_QK_DIM == MXU_COLS
    m, d = x_bf.shape
    n = w_in.shape[2]
    half = RET_QK_DIM // 2
    pos_blocks = seq // tm
    kern = functools.partial(_ret_in_kernel, qk_width=qk_width, v_width=v_width,
                             k_scale=RET_QK_DIM ** -0.5)
    return pl.pallas_call(
        kern,
        grid=(m // tm, n // tn),
        in_specs=[
            pl.BlockSpec((tm, d), lambda i, j: (i, 0), pipeline_mode=pl.Buffered(1)),
            pl.BlockSpec((None, d, tn), lambda i, j: (layer, 0, j)),
            pl.BlockSpec((tm, half), lambda i, j: (i % pos_blocks, 0)),
            pl.BlockSpec((tm, half), lambda i, j: (i % pos_blocks, 0)),
        ],
        out_specs=pl.BlockSpec((tm, tn), lambda i, j: (i, j)),
        out_shape=jax.ShapeDtypeStruct((m, n), BF16),
        compiler_params=_cparams(("parallel", "arbitrary")),
        name="ret_in_proj",
    )(x_bf, w_in, cos, sin)


N_SPLIT = 3


def _split3(v):
    hi = v.astype(BF16)
    r = v - hi.astype(F32)
    mid = r.astype(BF16)
    lo = (r - mid.astype(F32)).astype(BF16)
    return hi, mid, lo


def _fgate_kernel(x_ref, w_ref, b_ref, c3_ref, carry_ref):
    ts = x_ref.shape[0]
    n_heads = w_ref.shape[0]

    @pl.when(pl.program_id(1) == 0)
    def _():
        carry_ref[...] = jnp.zeros_like(carry_ref)

    z = lax.dot_general(x_ref[...], w_ref[...].astype(BF16), NT_DIMS,
                        preferred_element_type=F32) + b_ref[...]
    log_f = jnp.minimum(z, 0.0) - jnp.log1p(jnp.exp(-jnp.abs(z)))
    row = lax.broadcasted_iota(jnp.int32, (ts, ts), 0)
    col = lax.broadcasted_iota(jnp.int32, (ts, ts), 1)
    tri = jnp.where(col <= row, 1.0, 0.0).astype(BF16)
    cs = carry_ref[...]
    for part in _split3(log_f):
        cs = cs + jnp.dot(tri, part, preferred_element_type=F32)
    carry_ref[...] = cs[ts - 1:ts, :]
    src = lax.broadcasted_iota(jnp.int32, (n_heads, LANES), 0)
    dst = lax.broadcasted_iota(jnp.int32, (n_heads, LANES), 1)
    c3 = jnp.zeros((ts, LANES), F32)
    for t, part in enumerate(_split3(cs * LOG2E)):
        place = jnp.where(dst == t * n_heads + src, 1.0, 0.0).astype(BF16)
        c3 = c3 + jnp.dot(part, place, preferred_element_type=F32)
    c3_ref[...] = c3.astype(c3_ref.dtype)


def _fox_forget_cumsum(x_bf, w_in_t, layer, b_f, batch, seq, f_col0, *, ts=512):
    d = x_bf.shape[1]
    h = b_f.shape[0]
    ns = seq // ts
    assert f_col0 % h == 0 and w_in_t.shape[1] == f_col0 + h and N_SPLIT * h <= LANES
    return pl.pallas_call(
        _fgate_kernel,
        grid=(batch, ns),
        in_specs=[
            pl.BlockSpec((ts, d), lambda bi, si: (bi * ns + si, 0)),
            pl.BlockSpec((None, h, d), lambda bi, si: (layer, f_col0 // h, 0)),
            pl.BlockSpec((1, h), lambda bi, si: (0, 0)),
        ],
        out_specs=pl.BlockSpec((ts, LANES), lambda bi, si: (bi * ns + si, 0)),
        out_shape=jax.ShapeDtypeStruct((batch * seq, LANES), BF16),
        scratch_shapes=[pltpu.VMEM((1, h), F32)],
        compiler_params=_cparams(("parallel", "arbitrary")),
        name="fox_forget_cumsum",
    )(x_bf, w_in_t, b_f.reshape(1, -1))


def _fox_attn_kernel(q_ref, k_ref, v_ref, g_ref, c3_ref, o_ref,
                     qa_ref, kaug_ref, vaug_ref, m_ref, acc_ref, p_ref, alpha_ref,
                     *, blk, n_heads):
    h = pl.program_id(1)
    s_len = k_ref.shape[0]
    dh = FOX_HEAD_DIM
    n_blk = s_len // blk

    src = lax.broadcasted_iota(jnp.int32, (LANES, 2 * dh), 0)
    dst = lax.broadcasted_iota(jnp.int32, (LANES, 2 * dh), 1)
    to_q = (dst < N_SPLIT) & (src == dst * n_heads + h)
    dk = dst - (dh + N_SPLIT)
    to_k = (dk >= 0) & (dk < N_SPLIT) & (src == dk * n_heads + h)
    place = (jnp.where(to_q, 1.0, 0.0) - jnp.where(to_k, 1.0, 0.0)).astype(BF16)
    x = jnp.dot(c3_ref[...], place, preferred_element_type=F32)
    lane = lax.broadcasted_iota(jnp.int32, (s_len, dh), 1)
    qa_ref[:, :dh] = q_ref[...]
    qa_ref[:, dh:] = (x[:, :dh] + jnp.where((lane >= N_SPLIT) & (lane < 2 * N_SPLIT), 1.0, 0.0)
                      ).astype(BF16)
    kaug_ref[:, :dh] = k_ref[...]
    kaug_ref[:, dh:] = (x[:, dh:] + jnp.where(lane < N_SPLIT, 1.0, 0.0)).astype(BF16)
    vaug_ref[:, :dh] = v_ref[...]
    vaug_ref[:, dh:] = jnp.ones((s_len, dh), BF16)
    m_ref[...] = jnp.full(m_ref.shape, -jnp.inf, F32)
    acc_ref[...] = jnp.zeros_like(acc_ref)

    def qk_softmax(slot, qi, ki):
        rows = pl.ds(qi * blk, blk)
        s = lax.dot_general(qa_ref[rows, :], kaug_ref[pl.ds(ki * blk, blk), :], NT_DIMS,
                            preferred_element_type=F32)
        if ki == qi:
            row = lax.broadcasted_iota(jnp.int32, (blk, blk), 0)
            col = lax.broadcasted_iota(jnp.int32, (blk, blk), 1)
            s = jnp.where(col <= row, s, -jnp.inf)
        m_old = m_ref[rows, :]
        m_new = jnp.maximum(m_old, jnp.max(s, axis=-1, keepdims=True))
        alpha_ref[slot] = jnp.exp2(m_old - m_new)
        p_ref[slot] = jnp.exp2(s - jnp.concatenate([m_new] * (blk // LANES), axis=1)).astype(BF16)
        m_ref[rows, :] = m_new

    def pv_update(slot, qi, ki):
        rows = pl.ds(qi * blk, blk)
        pv = jnp.dot(p_ref[slot], vaug_ref[pl.ds(ki * blk, blk), :], preferred_element_type=F32)
        alpha = alpha_ref[slot]
        acc_ref[rows, :] = jnp.concatenate([alpha, alpha], axis=1) * acc_ref[rows, :] + pv
        if ki == qi:
            o_ref[rows, :] = (acc_ref[rows, :dh] / acc_ref[rows, dh:]
                              * g_ref[rows, :].astype(F32)).astype(o_ref.dtype)

    pairs = [(qi, ki) for qi in range(n_blk) for ki in range(qi + 1)]
    for t, pair in enumerate(pairs):
        qk_softmax(t % 2, *pair)
        if t > 0:
            pv_update((t - 1) % 2, *pairs[t - 1])
    pv_update((len(pairs) - 1) % 2, *pairs[-1])


def _fox_attention(proj, c3, batch, seq, *, blk=512):
    m, n = proj.shape
    width = n // 4
    heads = width // FOX_HEAD_DIM
    dh = FOX_HEAD_DIM
    kern = functools.partial(_fox_attn_kernel, blk=blk, n_heads=heads)
    return pl.pallas_call(
        kern,
        grid=(batch, heads),
        in_specs=[
            pl.BlockSpec((seq, dh), lambda b, h: (b, h)),
            pl.BlockSpec((seq, dh), lambda b, h: (b, heads + h)),
            pl.BlockSpec((seq, dh), lambda b, h: (b, 2 * heads + h)),
            pl.BlockSpec((seq, dh), lambda b, h: (b, 3 * heads + h)),
            pl.BlockSpec((seq, LANES), lambda b, h: (b, 0)),
        ],
        out_specs=pl.BlockSpec((seq, dh), lambda b, h: (b, h)),
        out_shape=jax.ShapeDtypeStruct((m, width), BF16),
        scratch_shapes=[
            pltpu.VMEM((seq, 2 * dh), BF16),
            pltpu.VMEM((seq, 2 * dh), BF16),
            pltpu.VMEM((seq, 2 * dh), BF16),
            pltpu.VMEM((seq, LANES), F32),
            pltpu.VMEM((seq, 2 * dh), F32),
            pltpu.VMEM((2, blk, blk), BF16),
            pltpu.VMEM((2, blk, LANES), F32),
        ],
        compiler_params=_cparams(("parallel", "parallel")),
        name="fox_attention",
    )(proj, proj, proj, proj, c3)


def _out_ln_kernel(y_ref, w_ref, h_ref, g_ref, b_ref, o_ref, obf_ref, *, alpha):
    br = jnp.dot(y_ref[...], w_ref[...], preferred_element_type=F32)
    z = alpha * h_ref[...] + br
    mu = jnp.mean(z, axis=-1, keepdims=True)
    zc = z - mu
    var = jnp.mean(zc * zc, axis=-1, keepdims=True)
    out = zc * lax.rsqrt(var + LN_EPS) * g_ref[...] + b_ref[...]
    o_ref[...] = out
    obf_ref[...] = out.astype(obf_ref.dtype)


def _out_proj_ln(y_bf, w_bf, layer, h, gain, bias, alpha, *, tm=256):
    m, k = y_bf.shape
    d = w_bf.shape[2]
    kern = functools.partial(_out_ln_kernel, alpha=alpha)
    return pl.pallas_call(
        kern,
        grid=(m // tm,),
        in_specs=[
            pl.BlockSpec((tm, k), lambda i: (i, 0)),
            pl.BlockSpec((None, k, d), lambda i: (layer, 0, 0)),
            pl.BlockSpec((tm, d), lambda i: (i, 0)),
            pl.BlockSpec((1, d), lambda i: (0, 0)),
            pl.BlockSpec((1, d), lambda i: (0, 0)),
        ],
        out_specs=[pl.BlockSpec((tm, d), lambda i: (i, 0)),
                   pl.BlockSpec((tm, d), lambda i: (i, 0))],
        out_shape=[jax.ShapeDtypeStruct((m, d), F32), jax.ShapeDtypeStruct((m, d), BF16)],
        compiler_params=_cparams(("parallel",)),
        name="out_proj_ln",
    )(y_bf, w_bf, h, gain.reshape(1, -1), bias.reshape(1, -1))


def _retention_kernel(lg_ref, q_ref, k_ref, v_ref, g_ref, o_ref, r_ref, *, chunk):
    h = pl.program_id(1)
    n = pl.program_id(2)

    @pl.when(n == 0)
    def _():
        r_ref[...] = jnp.zeros_like(r_ref)

    lg = lg_ref[h]
    row = lax.broadcasted_iota(jnp.int32, (chunk, chunk), 0)
    col = lax.broadcasted_iota(jnp.int32, (chunk, chunk), 1)
    diff = (row - col).astype(F32)
    intra = jnp.where(diff >= 0, jnp.exp(jnp.maximum(diff, 0.0) * lg), 0.0)
    pos = lax.broadcasted_iota(jnp.int32, (chunk, 1), 0).astype(F32)
    q_decay = jnp.exp((pos + 1.0) * lg)
    k_decay = jnp.exp((chunk - 1.0 - pos) * lg)
    c_decay = jnp.exp(jnp.full((1, 1), chunk, F32) * lg)

    r = r_ref[...]
    for ci in range(q_ref.shape[0] // chunk):
        rows = slice(ci * chunk, (ci + 1) * chunk)
        q = q_ref[rows, :]
        k = k_ref[rows, :]
        v = v_ref[rows, :]
        inner = lax.dot_general(q, k, NT_DIMS, preferred_element_type=F32) * intra
        o = (jnp.dot(inner.astype(BF16), v, preferred_element_type=F32)
             + jnp.dot(q, r.astype(BF16), preferred_element_type=F32) * q_decay)
        kd = (k.astype(F32) * k_decay).astype(BF16)
        r = r * c_decay + lax.dot_general(kd, v, TN_DIMS, preferred_element_type=F32)

        mu = jnp.mean(o, axis=-1, keepdims=True)
        oc = o - mu
        var = jnp.mean(oc * oc, axis=-1, keepdims=True)
        o_ref[rows, :] = (oc * lax.rsqrt(var + GN_EPS)
                          * g_ref[rows, :].astype(F32)).astype(o_ref.dtype)
    r_ref[...] = r


def _retention(proj, log_gamma, batch, seq, heads, *, chunk=256, chunks_per_step=4):
    m = proj.shape[0]
    ts = chunk * chunks_per_step
    ns = seq // ts
    dk, dv = RET_QK_DIM, RET_V_DIM
    v_off = 2 * heads * dk // dv
    g_off = v_off + heads
    kern = functools.partial(_retention_kernel, chunk=chunk)
    return pl.pallas_call(
        kern,
        grid_spec=pltpu.PrefetchScalarGridSpec(
            num_scalar_prefetch=1,
            grid=(batch, heads, ns),
            in_specs=[
                pl.BlockSpec((ts, dk), lambda b, h, n, lg: (b * ns + n, h)),
                pl.BlockSpec((ts, dk), lambda b, h, n, lg: (b * ns + n, heads + h)),
                pl.BlockSpec((ts, dv), lambda b, h, n, lg: (b * ns + n, v_off + h)),
                pl.BlockSpec((ts, dv), lambda b, h, n, lg: (b * ns + n, g_off + h)),
            ],
            out_specs=pl.BlockSpec((ts, dv), lambda b, h, n, lg: (b * ns + n, h)),
            scratch_shapes=[pltpu.VMEM((dk, dv), F32)],
        ),
        out_shape=jax.ShapeDtypeStruct((m, heads * dv), BF16),
        compiler_params=_cparams(("parallel", "parallel", "arbitrary")),
        name="retention",
    )(log_gamma, proj, proj, proj, proj)


def kernel(x, fox_w_in, fox_b_f, fox_q_gain, fox_k_gain, fox_w_out, ret_w_in, ret_w_out,
           ln_gain, ln_bias):
    batch, seq, d_model = x.shape
    depth = ln_gain.shape[0]
    m = batch * seq
    alpha = (2.0 * depth) ** 0.25

    fox_width = fox_w_out.shape[1]
    ret_v_width = ret_w_out.shape[1]
    ret_heads = ret_v_width // RET_V_DIM
    ret_qk_width = ret_heads * RET_QK_DIM

    inv_freq = ROPE_BASE ** (-jnp.arange(0, RET_QK_DIM, 2, dtype=F32) / RET_QK_DIM)
    ang = jnp.arange(seq, dtype=F32)[:, None] * inv_freq[None, :]
    cos, sin = jnp.cos(ang), jnp.sin(ang)
    log_gamma = jnp.log1p(-jnp.exp2(-5.0 - jnp.arange(ret_heads, dtype=F32)))

    fox_w_in_t = jnp.swapaxes(fox_w_in, 1, 2)
    fox_w_out_bf = fox_w_out.astype(BF16)
    ret_w_out_bf = ret_w_out.astype(BF16)
    h = x.reshape(m, d_model)
    h_bf = h.astype(BF16)
    for i in range(depth):
        j = i // 2
        if i % 2 == 0:
            proj = _fox_in_proj(h_bf, fox_w_in_t, j, fox_q_gain[j], fox_k_gain[j], fox_width)
            c3 = _fox_forget_cumsum(h_bf, fox_w_in_t, j, fox_b_f[j], batch, seq, 4 * fox_width)
            y = _fox_attention(proj, c3, batch, seq)
            w_out = fox_w_out_bf
        else:
            proj = _ret_in_proj(h_bf, ret_w_in, j, cos, sin, seq, ret_qk_width, ret_v_width)
            y = _retention(proj, log_gamma, batch, seq, ret_heads)
            w_out = ret_w_out_bf
        h, h_bf = _out_proj_ln(y, w_out, j, h, ln_gain[i], ln_bias[i], alpha)
    return h.reshape(batch, seq, d_model)
```

```python
import functools
import math

import jax
import jax.numpy as jnp
from jax import lax
from jax.experimental import pallas as pl
from jax.experimental.pallas import tpu as pltpu

F32 = jnp.float32
BF16 = jnp.bfloat16

LANES = 128
MXU_COLS = 256

FOX_HEAD_DIM = 128
RET_QK_DIM = 256
RET_V_DIM = 512
ROPE_BASE = 10000.0
LN_EPS = 1e-5
GN_EPS = 1e-6
QK_EPS = 1e-6
LOG2E = math.log2(math.e)

VMEM_LIMIT = 56 * 1024 * 1024

NT_DIMS = (((1,), (1,)), ((), ()))
TN_DIMS = (((0,), (0,)), ((), ()))


def _cparams(sem):
    return pltpu.CompilerParams(dimension_semantics=sem, vmem_limit_bytes=VMEM_LIMIT)


def _silu(g):
    return g * (1.0 / (1.0 + jnp.exp(-g)))


def _dot_by_column_groups(x_ref, w_ref, o_ref, epilogue):
    tn = o_ref.shape[1]
    for c in range(tn // MXU_COLS):
        cols = slice(c * MXU_COLS, (c + 1) * MXU_COLS)
        acc = jnp.dot(x_ref[...], w_ref[:, cols].astype(BF16), preferred_element_type=F32)
        o_ref[:, cols] = epilogue(acc).astype(o_ref.dtype)


def _fox_in_kernel(x_ref, w_ref, qg_ref, kg_ref, o_ref, *, width, q_scale):
    j = pl.program_id(1)
    per_region = width // o_ref.shape[1]

    def rms(g_ref, scale):
        def epilogue(acc):
            outs = []
            for hh in range(MXU_COLS // FOX_HEAD_DIM):
                a = acc[:, hh * FOX_HEAD_DIM:(hh + 1) * FOX_HEAD_DIM]
                ms = jnp.mean(a * a, axis=-1, keepdims=True)
                y = a * lax.rsqrt(ms + QK_EPS) * g_ref[...]
                outs.append(y if scale is None else y * scale)
            return jnp.concatenate(outs, axis=1)
        return epilogue

    @pl.when(j < per_region)
    def _():
        _dot_by_column_groups(x_ref, w_ref, o_ref, rms(qg_ref, q_scale))

    @pl.when((j >= per_region) & (j < 2 * per_region))
    def _():
        _dot_by_column_groups(x_ref, w_ref, o_ref, rms(kg_ref, None))

    @pl.when((j >= 2 * per_region) & (j < 3 * per_region))
    def _():
        _dot_by_column_groups(x_ref, w_ref, o_ref, lambda acc: acc)

    @pl.when(j >= 3 * per_region)
    def _():
        _dot_by_column_groups(x_ref, w_ref, o_ref, _silu)


def _fox_in_proj(x_bf, w_in, layer, q_gain, k_gain, width, *, tm=2048, tn=512):
    m, d = x_bf.shape
    n = 4 * width
    kern = functools.partial(_fox_in_kernel, width=width, q_scale=FOX_HEAD_DIM ** -0.5 * LOG2E)
    return pl.pallas_call(
        kern,
        grid=(m // tm, n // tn),
        in_specs=[
            pl.BlockSpec((tm, d), lambda i, j: (i, 0)),
            pl.BlockSpec((None, d, tn), lambda i, j: (layer, 0, j)),
            pl.BlockSpec((1, FOX_HEAD_DIM), lambda i, j: (0, 0)),
            pl.BlockSpec((1, FOX_HEAD_DIM), lambda i, j: (0, 0)),
        ],
        out_specs=pl.BlockSpec((tm, tn), lambda i, j: (i, j)),
        out_shape=jax.ShapeDtypeStruct((m, n), BF16),
        compiler_params=_cparams(("parallel", "arbitrary")),
        name="fox_in_proj",
    )(x_bf, w_in, q_gain.reshape(1, -1), k_gain.reshape(1, -1))


def _ret_in_kernel(x_ref, w_ref, cos_ref, sin_ref, o_ref, *, qk_width, v_width, k_scale):
    j = pl.program_id(1)
    tn = o_ref.shape[1]
    half = RET_QK_DIM // 2
    nq = qk_width // tn
    nv = v_width // tn

    def rope(scale):
        def epilogue(acc):
            cos = cos_ref[...]
            sin = sin_ref[...]
            t1 = acc[:, :half]
            t2 = acc[:, half:]
            r = jnp.concatenate([t1 * cos - t2 * sin, t1 * sin + t2 * cos], axis=1)
            return r if scale is None else r * scale
        return epilogue

    @pl.when(j < nq)
    def _():
        _dot_by_column_groups(x_ref, w_ref, o_ref, rope(None))

    @pl.when((j >= nq) & (j < 2 * nq))
    def _():
        _dot_by_column_groups(x_ref, w_ref, o_ref, rope(k_scale))

    @pl.when((j >= 2 * nq) & (j < 2 * nq + nv))
    def _():
        _dot_by_column_groups(x_ref, w_ref, o_ref, lambda acc: acc)

    @pl.when(j >= 2 * nq + nv)
    def _():
        _dot_by_column_groups(x_ref, w_ref, o_ref, _silu)


def _ret_in_proj(x_bf, w_in, layer, cos, sin, seq, qk_width, v_width, *, tm=2048, tn=512):
    assert RET_QK_DIM == MXU_COLS
    m, d = x_bf.shape
    n = w_in.shape[2]
    half = RET_QK_DIM // 2
    pos_blocks = seq // tm
    kern = functools.partial(_ret_in_kernel, qk_width=qk_width, v_width=v_width,
                             k_scale=RET_QK_DIM ** -0.5)
    return pl.pallas_call(
        kern,
        grid=(m // tm, n // tn),
        in_specs=[
            pl.BlockSpec((tm, d), lambda i, j: (i, 0)),
            pl.BlockSpec((None, d, tn), lambda i, j: (layer, 0, j)),
            pl.BlockSpec((tm, half), lambda i, j: (i % pos_blocks, 0)),
            pl.BlockSpec((tm, half), lambda i, j: (i % pos_blocks, 0)),
        ],
        out_specs=pl.BlockSpec((tm, tn), lambda i, j: (i, j)),
        out_shape=jax.ShapeDtypeStruct((m, n), BF16),
        compiler_params=_cparams(("parallel", "arbitrary")),
        name="ret_in_proj",
    )(x_bf, w_in, cos, sin)


N_SPLIT = 3


def _split3(v):
    hi = v.astype(BF16)
    r = v - hi.astype(F32)
    mid = r.astype(BF16)
    lo = (r - mid.astype(F32)).astype(BF16)
    return hi, mid, lo


def _fgate_kernel(x_ref, w_ref, b_ref, c3_ref, carry_ref):
    ts = x_ref.shape[0]
    n_heads = w_ref.shape[0]

    @pl.when(pl.program_id(1) == 0)
    def _():
        carry_ref[...] = jnp.zeros_like(carry_ref)

    z = lax.dot_general(x_ref[...], w_ref[...].astype(BF16), NT_DIMS,
                        preferred_element_type=F32) + b_ref[...]
    log_f = jnp.minimum(z, 0.0) - jnp.log1p(jnp.exp(-jnp.abs(z)))
    row = lax.broadcasted_iota(jnp.int32, (ts, ts), 0)
    col = lax.broadcasted_iota(jnp.int32, (ts, ts), 1)
    tri = jnp.where(col <= row, 1.0, 0.0).astype(BF16)
    cs = carry_ref[...]
    for part in _split3(log_f):
        cs = cs + jnp.dot(tri, part, preferred_element_type=F32)
    carry_ref[...] = cs[ts - 1:ts, :]
    src = lax.broadcasted_iota(jnp.int32, (n_heads, LANES), 0)
    dst = lax.broadcasted_iota(jnp.int32, (n_heads, LANES), 1)
    c3 = jnp.zeros((ts, LANES), F32)
    for t, part in enumerate(_split3(cs * LOG2E)):
        place = jnp.where(dst == t * n_heads + src, 1.0, 0.0).astype(BF16)
        c3 = c3 + jnp.dot(part, place, preferred_element_type=F32)
    c3_ref[...] = c3.astype(c3_ref.dtype)


def _fox_forget_cumsum(x_bf, w_in_t, layer, b_f, batch, seq, f_col0, *, ts=512):
    d = x_bf.shape[1]
    h = b_f.shape[0]
    ns = seq // ts
    assert f_col0 % h == 0 and w_in_t.shape[1] == f_col0 + h and N_SPLIT * h <= LANES
    return pl.pallas_call(
        _fgate_kernel,
        grid=(batch, ns),
        in_specs=[
            pl.BlockSpec((ts, d), lambda bi, si: (bi * ns + si, 0)),
            pl.BlockSpec((None, h, d), lambda bi, si: (layer, f_col0 // h, 0)),
            pl.BlockSpec((1, h), lambda bi, si: (0, 0)),
        ],
        out_specs=pl.BlockSpec((ts, LANES), lambda bi, si: (bi * ns + si, 0)),
        out_shape=jax.ShapeDtypeStruct((batch * seq, LANES), BF16),
        scratch_shapes=[pltpu.VMEM((1, h), F32)],
        compiler_params=_cparams(("parallel", "arbitrary")),
        name="fox_forget_cumsum",
    )(x_bf, w_in_t, b_f.reshape(1, -1))


def _fox_attn_kernel(q_ref, k_ref, v_ref, g_ref, c3_ref, o_ref,
                     qa_ref, kaug_ref, vaug_ref, m_ref, acc_ref, p_ref, alpha_ref,
                     *, blk, n_heads):
    h = pl.program_id(1)
    s_len = k_ref.shape[0]
    dh = FOX_HEAD_DIM
    n_blk = s_len // blk

    src = lax.broadcasted_iota(jnp.int32, (LANES, 2 * dh), 0)
    dst = lax.broadcasted_iota(jnp.int32, (LANES, 2 * dh), 1)
    to_q = (dst < N_SPLIT) & (src == dst * n_heads + h)
    dk = dst - (dh + N_SPLIT)
    to_k = (dk >= 0) & (dk < N_SPLIT) & (src == dk * n_heads + h)
    place = (jnp.where(to_q, 1.0, 0.0) - jnp.where(to_k, 1.0, 0.0)).astype(BF16)
    x = jnp.dot(c3_ref[...], place, preferred_element_type=F32)
    lane = lax.broadcasted_iota(jnp.int32, (s_len, dh), 1)
    qa_ref[:, :dh] = q_ref[...]
    qa_ref[:, dh:] = (x[:, :dh] + jnp.where((lane >= N_SPLIT) & (lane < 2 * N_SPLIT), 1.0, 0.0)
                      ).astype(BF16)
    kaug_ref[:, :dh] = k_ref[...]
    kaug_ref[:, dh:] = (x[:, dh:] + jnp.where(lane < N_SPLIT, 1.0, 0.0)).astype(BF16)
    vaug_ref[:, :dh] = v_ref[...]
    vaug_ref[:, dh:] = jnp.ones((s_len, dh), BF16)
    m_ref[...] = jnp.full(m_ref.shape, -jnp.inf, F32)
    acc_ref[...] = jnp.zeros_like(acc_ref)

    def qk_softmax(slot, qi, ki):
        rows = pl.ds(qi * blk, blk)
        s = lax.dot_general(qa_ref[rows, :], kaug_ref[pl.ds(ki * blk, blk), :], NT_DIMS,
                            preferred_element_type=F32)
        if ki == qi:
            row = lax.broadcasted_iota(jnp.int32, (blk, blk), 0)
            col = lax.broadcasted_iota(jnp.int32, (blk, blk), 1)
            s = jnp.where(col <= row, s, -jnp.inf)
        m_old = m_ref[rows, :]
        m_new = jnp.maximum(m_old, jnp.max(s, axis=-1, keepdims=True))
        alpha_ref[slot] = jnp.exp2(m_old - m_new)
        p_ref[slot] = jnp.exp2(s - jnp.concatenate([m_new] * (blk // LANES), axis=1)).astype(BF16)
        m_ref[rows, :] = m_new

    def pv_update(slot, qi, ki):
        rows = pl.ds(qi * blk, blk)
        pv = jnp.dot(p_ref[slot], vaug_ref[pl.ds(ki * blk, blk), :], preferred_element_type=F32)
        alpha = alpha_ref[slot]
        acc_ref[rows, :] = jnp.concatenate([alpha, alpha], axis=1) * acc_ref[rows, :] + pv
        if ki == qi:
            o_ref[rows, :] = (acc_ref[rows, :dh] / acc_ref[rows, dh:]
                              * g_ref[rows, :].astype(F32)).astype(o_ref.dtype)

    pairs = [(qi, ki) for qi in range(n_blk) for ki in range(qi + 1)]
    for t, pair in enumerate(pairs):
        qk_softmax(t % 2, *pair)
        if t > 0:
            pv_update((t - 1) % 2, *pairs[t - 1])
    pv_update((len(pairs) - 1) % 2, *pairs[-1])


def _fox_attention(proj, c3, batch, seq, *, blk=512):
    m, n = proj.shape
    width = n // 4
    heads = width // FOX_HEAD_DIM
    dh = FOX_HEAD_DIM
    kern = functools.partial(_fox_attn_kernel, blk=blk, n_heads=heads)
    return pl.pallas_call(
        kern,
        grid=(batch, heads),
        in_specs=[
            pl.BlockSpec((seq, dh), lambda b, h: (b, h)),
            pl.BlockSpec((seq, dh), lambda b, h: (b, heads + h)),
            pl.BlockSpec((seq, dh), lambda b, h: (b, 2 * heads + h)),
            pl.BlockSpec((seq, dh), lambda b, h: (b, 3 * heads + h)),
            pl.BlockSpec((seq, LANES), lambda b, h: (b, 0)),
        ],
        out_specs=pl.BlockSpec((seq, dh), lambda b, h: (b, h)),
        out_shape=jax.ShapeDtypeStruct((m, width), BF16),
        scratch_shapes=[
            pltpu.VMEM((seq, 2 * dh), BF16),
            pltpu.VMEM((seq, 2 * dh), BF16),
            pltpu.VMEM((seq, 2 * dh), BF16),
            pltpu.VMEM((seq, LANES), F32),
            pltpu.VMEM((seq, 2 * dh), F32),
            pltpu.VMEM((2, blk, blk), BF16),
            pltpu.VMEM((2, blk, LANES), F32),
        ],
        compiler_params=_cparams(("parallel", "parallel")),
        name="fox_attention",
    )(proj, proj, proj, proj, c3)


def _out_ln_kernel(y_ref, w_ref, h_ref, g_ref, b_ref, o_ref, obf_ref, *, alpha, row_group):
    for r in range(0, y_ref.shape[0], row_group):
        rows = slice(r, r + row_group)
        br = jnp.dot(y_ref[rows, :], w_ref[...], preferred_element_type=F32)
        z = alpha * h_ref[rows, :] + br
        mu = jnp.mean(z, axis=-1, keepdims=True)
        zc = z - mu
        var = jnp.mean(zc * zc, axis=-1, keepdims=True)
        out = zc * lax.rsqrt(var + LN_EPS) * g_ref[...] + b_ref[...]
        o_ref[rows, :] = out
        obf_ref[rows, :] = out.astype(obf_ref.dtype)


def _out_proj_ln(y_bf, w_bf, layer, h, gain, bias, alpha, *, tm=512, row_group=256):
    m, k = y_bf.shape
    d = w_bf.shape[2]
    kern = functools.partial(_out_ln_kernel, alpha=alpha, row_group=row_group)
    return pl.pallas_call(
        kern,
        grid=(m // tm,),
        in_specs=[
            pl.BlockSpec((tm, k), lambda i: (i, 0)),
            pl.BlockSpec((None, k, d), lambda i: (layer, 0, 0), pipeline_mode=pl.Buffered(1)),
            pl.BlockSpec((tm, d), lambda i: (i, 0)),
            pl.BlockSpec((1, d), lambda i: (0, 0)),
            pl.BlockSpec((1, d), lambda i: (0, 0)),
        ],
        out_specs=[pl.BlockSpec((tm, d), lambda i: (i, 0)),
                   pl.BlockSpec((tm, d), lambda i: (i, 0))],
        out_shape=[jax.ShapeDtypeStruct((m, d), F32), jax.ShapeDtypeStruct((m, d), BF16)],
        compiler_params=_cparams(("parallel",)),
        name="out_proj_ln",
    )(y_bf, w_bf, h, gain.reshape(1, -1), bias.reshape(1, -1))


def _retention_kernel(lg_ref, q_ref, k_ref, v_ref, g_ref, o_ref, r_ref, *, chunk):
    h = pl.program_id(1)
    n = pl.program_id(2)

    @pl.when(n == 0)
    def _():
        r_ref[...] = jnp.zeros_like(r_ref)

    lg = lg_ref[h]
    row = lax.broadcasted_iota(jnp.int32, (chunk, chunk), 0)
    col = lax.broadcasted_iota(jnp.int32, (chunk, chunk), 1)
    diff = (row - col).astype(F32)
    intra = jnp.where(diff >= 0, jnp.exp(jnp.maximum(diff, 0.0) * lg), 0.0)
    pos = lax.broadcasted_iota(jnp.int32, (chunk, 1), 0).astype(F32)
    q_decay = jnp.exp((pos + 1.0) * lg)
    k_decay = jnp.exp((chunk - 1.0 - pos) * lg)
    c_decay = jnp.exp(jnp.full((1, 1), chunk, F32) * lg)

    r = r_ref[...]
    for ci in range(q_ref.shape[0] // chunk):
        rows = slice(ci * chunk, (ci + 1) * chunk)
        q = q_ref[rows, :]
        k = k_ref[rows, :]
        v = v_ref[rows, :]
        inner = lax.dot_general(q, k, NT_DIMS, preferred_element_type=F32) * intra
        o = (jnp.dot(inner.astype(BF16), v, preferred_element_type=F32)
             + jnp.dot(q, r.astype(BF16), preferred_element_type=F32) * q_decay)
        kd = (k.astype(F32) * k_decay).astype(BF16)
        r = r * c_decay + lax.dot_general(kd, v, TN_DIMS, preferred_element_type=F32)

        mu = jnp.mean(o, axis=-1, keepdims=True)
        oc = o - mu
        var = jnp.mean(oc * oc, axis=-1, keepdims=True)
        o_ref[rows, :] = (oc * lax.rsqrt(var + GN_EPS)
                          * g_ref[rows, :].astype(F32)).astype(o_ref.dtype)
    r_ref[...] = r


def _retention(proj, log_gamma, batch, seq, heads, *, chunk=256, chunks_per_step=4):
    m = proj.shape[0]
    ts = chunk * chunks_per_step
    ns = seq // ts
    dk, dv = RET_QK_DIM, RET_V_DIM
    v_off = 2 * heads * dk // dv
    g_off = v_off + heads
    kern = functools.partial(_retention_kernel, chunk=chunk)
    return pl.pallas_call(
        kern,
        grid_spec=pltpu.PrefetchScalarGridSpec(
            num_scalar_prefetch=1,
            grid=(batch, heads, ns),
            in_specs=[
                pl.BlockSpec((ts, dk), lambda b, h, n, lg: (b * ns + n, h)),
                pl.BlockSpec((ts, dk), lambda b, h, n, lg: (b * ns + n, heads + h)),
                pl.BlockSpec((ts, dv), lambda b, h, n, lg: (b * ns + n, v_off + h)),
                pl.BlockSpec((ts, dv), lambda b, h, n, lg: (b * ns + n, g_off + h)),
            ],
            out_specs=pl.BlockSpec((ts, dv), lambda b, h, n, lg: (b * ns + n, h)),
            scratch_shapes=[pltpu.VMEM((dk, dv), F32)],
        ),
        out_shape=jax.ShapeDtypeStruct((m, heads * dv), BF16),
        compiler_params=_cparams(("parallel", "parallel", "arbitrary")),
        name="retention",
    )(log_gamma, proj, proj, proj, proj)


def kernel(x, fox_w_in, fox_b_f, fox_q_gain, fox_k_gain, fox_w_out, ret_w_in, ret_w_out,
           ln_gain, ln_bias):
    batch, seq, d_model = x.shape
    depth = ln_gain.shape[0]
    m = batch * seq
    alpha = (2.0 * depth) ** 0.25

    fox_width = fox_w_out.shape[1]
    ret_v_width = ret_w_out.shape[1]
    ret_heads = ret_v_width // RET_V_DIM
    ret_qk_width = ret_heads * RET_QK_DIM

    inv_freq = ROPE_BASE ** (-jnp.arange(0, RET_QK_DIM, 2, dtype=F32) / RET_QK_DIM)
    ang = jnp.arange(seq, dtype=F32)[:, None] * inv_freq[None, :]
    cos, sin = jnp.cos(ang), jnp.sin(ang)
    log_gamma = jnp.log1p(-jnp.exp2(-5.0 - jnp.arange(ret_heads, dtype=F32)))

    fox_w_in_t = jnp.swapaxes(fox_w_in, 1, 2)
    fox_w_in_bf = fox_w_in.astype(BF16)
    fox_w_out_bf = fox_w_out.astype(BF16)
    ret_w_out_bf = ret_w_out.astype(BF16)
    h = x.reshape(m, d_model)
    h_bf = h.astype(BF16)
    for i in range(depth):
        j = i // 2
        if i % 2 == 0:
            proj = _fox_in_proj(h_bf, fox_w_in_bf, j, fox_q_gain[j], fox_k_gain[j], fox_width)
            c3 = _fox_forget_cumsum(h_bf, fox_w_in_t, j, fox_b_f[j], batch, seq, 4 * fox_width)
            y = _fox_attention(proj, c3, batch, seq)
            w_out = fox_w_out_bf
        else:
            proj = _ret_in_proj(h_bf, ret_w_in, j, cos, sin, seq, ret_qk_width, ret_v_width)
            y = _retention(proj, log_gamma, batch, seq, ret_heads)
            w_out = ret_w_out_bf
        h, h_bf = _out_proj_ln(y, w_out, j, h, ln_gain[i], ln_bias[i], alpha)
    return h.reshape(batch, seq, d_model)
```

```python
import functools
import math

import jax
import jax.numpy as jnp
from jax import lax
from jax.experimental import pallas as pl
from jax.experimental.pallas import tpu as pltpu

F32 = jnp.float32
BF16 = jnp.bfloat16

LANES = 128
MXU_COLS = 256
IN_PROJ_ROW_CHUNK = 512
EPILOGUE_ROWS = 32

FOX_HEAD_DIM = 128
RET_QK_DIM = 256
RET_V_DIM = 512
ROPE_BASE = 10000.0
LN_EPS = 1e-5
GN_EPS = 1e-6
QK_EPS = 1e-6
LOG2E = math.log2(math.e)

VMEM_LIMIT = 56 * 1024 * 1024

NT_DIMS = (((1,), (1,)), ((), ()))
TN_DIMS = (((0,), (0,)), ((), ()))


def _cparams(sem):
    return pltpu.CompilerParams(dimension_semantics=sem, vmem_limit_bytes=VMEM_LIMIT)


def _silu(g):
    return g * (1.0 / (1.0 + jnp.exp(-g)))


def _lagged_tile(t, n_tiles):
    return jnp.minimum(t, n_tiles - 1), jnp.maximum(t, 1) - 1


def _in_proj_body(x_ref, w_ref, o_ref, acc_ref, wb_ref, mixed_epilogue, *,
                  col_tiles, mixed_tiles, plain_tiles):
    t = pl.program_id(0)
    n_tiles = pl.num_programs(0) - 1
    _, done = _lagged_tile(t, n_tiles)
    jd = done % col_tiles
    tm, tn = o_ref.shape

    def step(mm_slot, done_slot, first_and_last):
        def run(do_matmul, epilogue):
            w_src = w_ref
            if do_matmul and w_ref.dtype != BF16:
                wb_ref[...] = w_ref[...].astype(BF16)
                w_src = wb_ref
            for r in range(0, tm, IN_PROJ_ROW_CHUNK):
                rows = slice(r, r + IN_PROJ_ROW_CHUNK)
                if do_matmul:
                    acc_ref[mm_slot, rows, :] = jnp.dot(x_ref[rows, :], w_src[...],
                                                        preferred_element_type=F32)
                if epilogue is not None:
                    for rr in range(r, r + IN_PROJ_ROW_CHUNK, EPILOGUE_ROWS):
                        sub = slice(rr, rr + EPILOGUE_ROWS)
                        for c in range(tn // MXU_COLS):
                            cols = slice(c * MXU_COLS, (c + 1) * MXU_COLS)
                            col = c * MXU_COLS
                            for piece in epilogue(acc_ref[done_slot, sub, cols], sub):
                                o_ref[sub, col:col + piece.shape[1]] = piece.astype(o_ref.dtype)
                                col += piece.shape[1]

        running = (t > 0) & (t < n_tiles)
        kinds = [(jd < mixed_tiles, mixed_epilogue),
                 ((jd >= mixed_tiles) & (jd < plain_tiles), lambda acc, rows: [acc]),
                 (jd >= plain_tiles, lambda acc, rows: [_silu(acc)])]
        for cond, epilogue in kinds:
            pl.when(running & cond)(functools.partial(run, True, epilogue))
        if first_and_last:
            pl.when(t == 0)(functools.partial(run, True, None))
            pl.when(t == n_tiles)(functools.partial(run, False, kinds[-1][1]))

    @pl.when(t % 2 == 0)
    def _():
        step(0, 1, True)

    @pl.when(t % 2 == 1)
    def _():
        step(1, 0, False)


def _fox_in_kernel(x_ref, w_ref, g_ref, o_ref, acc_ref, wb_ref, **tiles):
    def rms(acc, rows):
        outs = []
        for hh in range(MXU_COLS // FOX_HEAD_DIM):
            a = acc[:, hh * FOX_HEAD_DIM:(hh + 1) * FOX_HEAD_DIM]
            ms = jnp.mean(a * a, axis=-1, keepdims=True)
            outs.append(a * lax.rsqrt(ms + QK_EPS) * g_ref[...])
        return outs

    _in_proj_body(x_ref, w_ref, o_ref, acc_ref, wb_ref, rms, **tiles)


def _in_proj_call(kern, name, x_bf, w_in, layer, aux, aux_specs, n, col_tiles, tm, tn):
    m, d = x_bf.shape
    n_tiles = (m // tm) * col_tiles
    assert n_tiles % 2 == 0 and n == col_tiles * tn

    def mm_tile(t):
        return _lagged_tile(t, n_tiles)[0]

    def done_tile(t):
        return _lagged_tile(t, n_tiles)[1]

    return pl.pallas_call(
        kern,
        grid=(n_tiles + 1,),
        in_specs=[
            pl.BlockSpec((tm, d), lambda t: (mm_tile(t) // col_tiles, 0)),
            pl.BlockSpec((None, d, tn), lambda t: (layer, 0, mm_tile(t) % col_tiles)),
        ] + [pl.BlockSpec(shape, functools.partial(lambda t, f: f(done_tile(t)), f=f))
             for shape, f in aux_specs],
        out_specs=pl.BlockSpec((tm, tn), lambda t: (done_tile(t) // col_tiles,
                                                    done_tile(t) % col_tiles)),
        out_shape=jax.ShapeDtypeStruct((m, n), BF16),
        scratch_shapes=[
            pltpu.VMEM((2, tm, tn), F32),
            pltpu.VMEM((d, tn) if w_in.dtype != BF16 else (16, LANES), BF16),
        ],
        compiler_params=_cparams(("arbitrary",)),
        name=name,
    )(x_bf, w_in, *aux)


def _fox_in_proj(x_bf, w_in, layer, q_gain, k_gain, width, *, tm=2048, tn=512):
    col_tiles = 4 * width // tn
    per_region = width // tn
    gains = jnp.stack([q_gain * (FOX_HEAD_DIM ** -0.5 * LOG2E), k_gain]).reshape(2, 1, -1)
    kern = functools.partial(_fox_in_kernel, col_tiles=col_tiles, mixed_tiles=2 * per_region,
                             plain_tiles=3 * per_region)
    gain_spec = ((None, 1, FOX_HEAD_DIM),
                 lambda done: (jnp.minimum((done % col_tiles) // per_region, 1), 0, 0))
    return _in_proj_call(kern, "fox_in_proj", x_bf, w_in, layer, [gains], [gain_spec],
                         4 * width, col_tiles, tm, tn)


def _ret_in_kernel(x_ref, w_ref, cos_ref, sin_ref, o_ref, acc_ref, wb_ref, **tiles):
    half = RET_QK_DIM // 2

    def rope(acc, rows):
        cos = cos_ref[rows, :]
        sin = sin_ref[rows, :]
        t1 = acc[:, :half]
        t2 = acc[:, half:]
        return [t1 * cos - t2 * sin, t1 * sin + t2 * cos]

    _in_proj_body(x_ref, w_ref, o_ref, acc_ref, wb_ref, rope, **tiles)


def _ret_in_proj(x_bf, w_in, layer, cos2, sin2, seq, qk_width, v_width, *, tm=2048, tn=512):
    assert RET_QK_DIM == MXU_COLS
    n = w_in.shape[2]
    col_tiles = n // tn
    q_tiles = qk_width // tn
    half = RET_QK_DIM // 2
    pos_blocks = seq // tm
    kern = functools.partial(_ret_in_kernel, col_tiles=col_tiles, mixed_tiles=2 * q_tiles,
                             plain_tiles=(2 * qk_width + v_width) // tn)
    table_spec = ((None, tm, half),
                  lambda done: (jnp.minimum((done % col_tiles) // q_tiles, 1),
                                (done // col_tiles) % pos_blocks, 0))
    return _in_proj_call(kern, "ret_in_proj", x_bf, w_in, layer, [cos2, sin2],
                         [table_spec, table_spec], n, col_tiles, tm, tn)


N_SPLIT = 3


def _split3(v):
    hi = v.astype(BF16)
    r = v - hi.astype(F32)
    mid = r.astype(BF16)
    lo = (r - mid.astype(F32)).astype(BF16)
    return hi, mid, lo


def _fgate_kernel(x_ref, w_ref, b_ref, c3_ref, carry_ref):
    ts = x_ref.shape[0]
    n_heads = w_ref.shape[0]

    @pl.when(pl.program_id(1) == 0)
    def _():
        carry_ref[...] = jnp.zeros_like(carry_ref)

    z = lax.dot_general(x_ref[...], w_ref[...].astype(BF16), NT_DIMS,
                        preferred_element_type=F32) + b_ref[...]
    log_f = jnp.minimum(z, 0.0) - jnp.log1p(jnp.exp(-jnp.abs(z)))
    row = lax.broadcasted_iota(jnp.int32, (ts, ts), 0)
    col = lax.broadcasted_iota(jnp.int32, (ts, ts), 1)
    tri = jnp.where(col <= row, 1.0, 0.0).astype(BF16)
    cs = carry_ref[...]
    for part in _split3(log_f):
        cs = cs + jnp.dot(tri, part, preferred_element_type=F32)
    carry_ref[...] = cs[ts - 1:ts, :]
    src = lax.broadcasted_iota(jnp.int32, (n_heads, LANES), 0)
    dst = lax.broadcasted_iota(jnp.int32, (n_heads, LANES), 1)
    c3 = jnp.zeros((ts, LANES), F32)
    for t, part in enumerate(_split3(cs * LOG2E)):
        place = jnp.where(dst == t * n_heads + src, 1.0, 0.0).astype(BF16)
        c3 = c3 + jnp.dot(part, place, preferred_element_type=F32)
    c3_ref[...] = c3.astype(c3_ref.dtype)


def _fox_forget_cumsum(x_bf, w_in_t, layer, b_f, batch, seq, f_col0, *, ts=512):
    d = x_bf.shape[1]
    h = b_f.shape[0]
    ns = seq // ts
    assert f_col0 % h == 0 and w_in_t.shape[1] == f_col0 + h and N_SPLIT * h <= LANES
    return pl.pallas_call(
        _fgate_kernel,
        grid=(batch, ns),
        in_specs=[
            pl.BlockSpec((ts, d), lambda bi, si: (bi * ns + si, 0)),
            pl.BlockSpec((None, h, d), lambda bi, si: (layer, f_col0 // h, 0)),
            pl.BlockSpec((1, h), lambda bi, si: (0, 0)),
        ],
        out_specs=pl.BlockSpec((ts, LANES), lambda bi, si: (bi * ns + si, 0)),
        out_shape=jax.ShapeDtypeStruct((batch * seq, LANES), BF16),
        scratch_shapes=[pltpu.VMEM((1, h), F32)],
        compiler_params=_cparams(("parallel", "arbitrary")),
        name="fox_forget_cumsum",
    )(x_bf, w_in_t, b_f.reshape(1, -1))


def _fox_attn_kernel(q_ref, k_ref, v_ref, g_ref, c3_ref, o_ref,
                     qa_ref, kaug_ref, vaug_ref, m_ref, acc_ref, p_ref, alpha_ref,
                     *, blk, n_heads):
    h = pl.program_id(1)
    s_len = k_ref.shape[0]
    dh = FOX_HEAD_DIM
    n_blk = s_len // blk

    src = lax.broadcasted_iota(jnp.int32, (LANES, 2 * dh), 0)
    dst = lax.broadcasted_iota(jnp.int32, (LANES, 2 * dh), 1)
    to_q = (dst < N_SPLIT) & (src == dst * n_heads + h)
    dk = dst - (dh + N_SPLIT)
    to_k = (dk >= 0) & (dk < N_SPLIT) & (src == dk * n_heads + h)
    place = (jnp.where(to_q, 1.0, 0.0) - jnp.where(to_k, 1.0, 0.0)).astype(BF16)
    x = jnp.dot(c3_ref[...], place, preferred_element_type=F32)
    lane = lax.broadcasted_iota(jnp.int32, (s_len, dh), 1)
    qa_ref[:, :dh] = q_ref[...]
    qa_ref[:, dh:] = (x[:, :dh] + jnp.where((lane >= N_SPLIT) & (lane < 2 * N_SPLIT), 1.0, 0.0)
                      ).astype(BF16)
    kaug_ref[:, :dh] = k_ref[...]
    kaug_ref[:, dh:] = (x[:, dh:] + jnp.where(lane < N_SPLIT, 1.0, 0.0)).astype(BF16)
    vaug_ref[:, :dh] = v_ref[...]
    vaug_ref[:, dh:] = jnp.ones((s_len, dh), BF16)
    m_ref[...] = jnp.full(m_ref.shape, -jnp.inf, F32)
    acc_ref[...] = jnp.zeros_like(acc_ref)

    def qk_softmax(slot, qi, ki):
        rows = pl.ds(qi * blk, blk)
        s = lax.dot_general(qa_ref[rows, :], kaug_ref[pl.ds(ki * blk, blk), :], NT_DIMS,
                            preferred_element_type=F32)
        if ki == qi:
            row = lax.broadcasted_iota(jnp.int32, (blk, blk), 0)
            col = lax.broadcasted_iota(jnp.int32, (blk, blk), 1)
            s = jnp.where(col <= row, s, -jnp.inf)
        m_old = m_ref[rows, :]
        m_new = jnp.maximum(m_old, jnp.max(s, axis=-1, keepdims=True))
        alpha_ref[slot] = jnp.exp2(m_old - m_new)
        p_ref[slot] = jnp.exp2(s - jnp.concatenate([m_new] * (blk // LANES), axis=1)).astype(BF16)
        m_ref[rows, :] = m_new

    def pv_update(slot, qi, ki):
        rows = pl.ds(qi * blk, blk)
        pv = jnp.dot(p_ref[slot], vaug_ref[pl.ds(ki * blk, blk), :], preferred_element_type=F32)
        alpha = alpha_ref[slot]
        acc_ref[rows, :] = jnp.concatenate([alpha, alpha], axis=1) * acc_ref[rows, :] + pv
        if ki == qi:
            o_ref[rows, :] = (acc_ref[rows, :dh] / acc_ref[rows, dh:]
                              * g_ref[rows, :].astype(F32)).astype(o_ref.dtype)

    pairs = [(qi, ki) for qi in range(n_blk) for ki in range(qi + 1)]
    for t, pair in enumerate(pairs):
        qk_softmax(t % 2, *pair)
        if t > 0:
            pv_update((t - 1) % 2, *pairs[t - 1])
    pv_update((len(pairs) - 1) % 2, *pairs[-1])


def _fox_attention(proj, c3, batch, seq, *, blk=512):
    m, n = proj.shape
    width = n // 4
    heads = width // FOX_HEAD_DIM
    dh = FOX_HEAD_DIM
    kern = functools.partial(_fox_attn_kernel, blk=blk, n_heads=heads)
    return pl.pallas_call(
        kern,
        grid=(batch, heads),
        in_specs=[
            pl.BlockSpec((seq, dh), lambda b, h: (b, h)),
            pl.BlockSpec((seq, dh), lambda b, h: (b, heads + h)),
            pl.BlockSpec((seq, dh), lambda b, h: (b, 2 * heads + h)),
            pl.BlockSpec((seq, dh), lambda b, h: (b, 3 * heads + h)),
            pl.BlockSpec((seq, LANES), lambda b, h: (b, 0)),
        ],
        out_specs=pl.BlockSpec((seq, dh), lambda b, h: (b, h)),
        out_shape=jax.ShapeDtypeStruct((m, width), BF16),
        scratch_shapes=[
            pltpu.VMEM((seq, 2 * dh), BF16),
            pltpu.VMEM((seq, 2 * dh), BF16),
            pltpu.VMEM((seq, 2 * dh), BF16),
            pltpu.VMEM((seq, LANES), F32),
            pltpu.VMEM((seq, 2 * dh), F32),
            pltpu.VMEM((2, blk, blk), BF16),
            pltpu.VMEM((2, blk, LANES), F32),
        ],
        compiler_params=_cparams(("parallel", "parallel")),
        name="fox_attention",
    )(proj, proj, proj, proj, c3)


def _out_ln_kernel(y_ref, w_ref, h_ref, g_ref, b_ref, o_ref, obf_ref, *, alpha, row_group):
    for r in range(0, y_ref.shape[0], row_group):
        rows = slice(r, r + row_group)
        br = jnp.dot(y_ref[rows, :], w_ref[...], preferred_element_type=F32)
        z = alpha * h_ref[rows, :] + br
        mu = jnp.mean(z, axis=-1, keepdims=True)
        zc = z - mu
        var = jnp.mean(zc * zc, axis=-1, keepdims=True)
        out = zc * lax.rsqrt(var + LN_EPS) * g_ref[...] + b_ref[...]
        o_ref[rows, :] = out
        obf_ref[rows, :] = out.astype(obf_ref.dtype)


def _out_proj_ln(y_bf, w_bf, layer, h, gain, bias, alpha, *, tm=512, row_group=256):
    m, k = y_bf.shape
    d = w_bf.shape[2]
    kern = functools.partial(_out_ln_kernel, alpha=alpha, row_group=row_group)
    return pl.pallas_call(
        kern,
        grid=(m // tm,),
        in_specs=[
            pl.BlockSpec((tm, k), lambda i: (i, 0)),
            pl.BlockSpec((None, k, d), lambda i: (layer, 0, 0), pipeline_mode=pl.Buffered(1)),
            pl.BlockSpec((tm, d), lambda i: (i, 0)),
            pl.BlockSpec((1, d), lambda i: (0, 0)),
            pl.BlockSpec((1, d), lambda i: (0, 0)),
        ],
        out_specs=[pl.BlockSpec((tm, d), lambda i: (i, 0)),
                   pl.BlockSpec((tm, d), lambda i: (i, 0))],
        out_shape=[jax.ShapeDtypeStruct((m, d), F32), jax.ShapeDtypeStruct((m, d), BF16)],
        compiler_params=_cparams(("parallel",)),
        name="out_proj_ln",
    )(y_bf, w_bf, h, gain.reshape(1, -1), bias.reshape(1, -1))


def _retention_kernel(lg_ref, q_ref, k_ref, v_ref, g_ref, o_ref, r_ref, *, chunk):
    h = pl.program_id(1)
    n = pl.program_id(2)

    @pl.when(n == 0)
    def _():
        r_ref[...] = jnp.zeros_like(r_ref)

    lg = lg_ref[h]
    row = lax.broadcasted_iota(jnp.int32, (chunk, chunk), 0)
    col = lax.broadcasted_iota(jnp.int32, (chunk, chunk), 1)
    diff = (row - col).astype(F32)
    intra = jnp.where(diff >= 0, jnp.exp(jnp.maximum(diff, 0.0) * lg), 0.0)
    pos = lax.broadcasted_iota(jnp.int32, (chunk, 1), 0).astype(F32)
    q_decay = jnp.exp((pos + 1.0) * lg)
    k_decay = jnp.exp((chunk - 1.0 - pos) * lg)
    c_decay = jnp.exp(jnp.full((1, 1), chunk, F32) * lg)

    r = r_ref[...]
    for ci in range(q_ref.shape[0] // chunk):
        rows = slice(ci * chunk, (ci + 1) * chunk)
        q = q_ref[rows, :]
        k = k_ref[rows, :]
        v = v_ref[rows, :]
        inner = lax.dot_general(q, k, NT_DIMS, preferred_element_type=F32) * intra
        o = (jnp.dot(inner.astype(BF16), v, preferred_element_type=F32)
             + jnp.dot(q, r.astype(BF16), preferred_element_type=F32) * q_decay)
        kd = (k.astype(F32) * k_decay).astype(BF16)
        r = r * c_decay + lax.dot_general(kd, v, TN_DIMS, preferred_element_type=F32)

        mu = jnp.mean(o, axis=-1, keepdims=True)
        oc = o - mu
        var = jnp.mean(oc * oc, axis=-1, keepdims=True)
        o_ref[rows, :] = (oc * lax.rsqrt(var + GN_EPS)
                          * g_ref[rows, :].astype(F32)).astype(o_ref.dtype)
    r_ref[...] = r


def _retention(proj, log_gamma, batch, seq, heads, *, chunk=256, chunks_per_step=4):
    m = proj.shape[0]
    ts = chunk * chunks_per_step
    ns = seq // ts
    dk, dv = RET_QK_DIM, RET_V_DIM
    v_off = 2 * heads * dk // dv
    g_off = v_off + heads
    kern = functools.partial(_retention_kernel, chunk=chunk)
    return pl.pallas_call(
        kern,
        grid_spec=pltpu.PrefetchScalarGridSpec(
            num_scalar_prefetch=1,
            grid=(batch, heads, ns),
            in_specs=[
                pl.BlockSpec((ts, dk), lambda b, h, n, lg: (b * ns + n, h)),
                pl.BlockSpec((ts, dk), lambda b, h, n, lg: (b * ns + n, heads + h)),
                pl.BlockSpec((ts, dv), lambda b, h, n, lg: (b * ns + n, v_off + h)),
                pl.BlockSpec((ts, dv), lambda b, h, n, lg: (b * ns + n, g_off + h)),
            ],
            out_specs=pl.BlockSpec((ts, dv), lambda b, h, n, lg: (b * ns + n, h)),
            scratch_shapes=[pltpu.VMEM((dk, dv), F32)],
        ),
        out_shape=jax.ShapeDtypeStruct((m, heads * dv), BF16),
        compiler_params=_cparams(("parallel", "parallel", "arbitrary")),
        name="retention",
    )(log_gamma, proj, proj, proj, proj)


def kernel(x, fox_w_in, fox_b_f, fox_q_gain, fox_k_gain, fox_w_out, ret_w_in, ret_w_out,
           ln_gain, ln_bias):
    batch, seq, d_model = x.shape
    depth = ln_gain.shape[0]
    m = batch * seq
    alpha = (2.0 * depth) ** 0.25

    fox_width = fox_w_out.shape[1]
    ret_v_width = ret_w_out.shape[1]
    ret_heads = ret_v_width // RET_V_DIM
    ret_qk_width = ret_heads * RET_QK_DIM

    inv_freq = ROPE_BASE ** (-jnp.arange(0, RET_QK_DIM, 2, dtype=F32) / RET_QK_DIM)
    ang = jnp.arange(seq, dtype=F32)[:, None] * inv_freq[None, :]
    k_scale = RET_QK_DIM ** -0.5
    cos2 = jnp.stack([jnp.cos(ang), jnp.cos(ang) * k_scale])
    sin2 = jnp.stack([jnp.sin(ang), jnp.sin(ang) * k_scale])
    log_gamma = jnp.log1p(-jnp.exp2(-5.0 - jnp.arange(ret_heads, dtype=F32)))

    fox_w_in_t = jnp.swapaxes(fox_w_in, 1, 2)
    fox_w_in_bf = fox_w_in.astype(BF16)
    fox_w_out_bf = fox_w_out.astype(BF16)
    ret_w_out_bf = ret_w_out.astype(BF16)
    h = x.reshape(m, d_model)
    h_bf = h.astype(BF16)
    for i in range(depth):
        j = i // 2
        if i % 2 == 0:
            proj = _fox_in_proj(h_bf, fox_w_in_bf, j, fox_q_gain[j], fox_k_gain[j], fox_width)
            c3 = _fox_forget_cumsum(h_bf, fox_w_in_t, j, fox_b_f[j], batch, seq, 4 * fox_width)
            y = _fox_attention(proj, c3, batch, seq)
            w_out = fox_w_out_bf
        else:
            proj = _ret_in_proj(h_bf, ret_w_in, j, cos2, sin2, seq, ret_qk_width, ret_v_width)
            y = _retention(proj, log_gamma, batch, seq, ret_heads)
            w_out = ret_w_out_bf
        h, h_bf = _out_proj_ln(y, w_out, j, h, ln_gain[i], ln_bias[i], alpha)
    return h.reshape(batch, seq, d_model)
```

```python
import functools
import math

import jax
import jax.numpy as jnp
from jax import lax
from jax.experimental import pallas as pl
from jax.experimental.pallas import tpu as pltpu

F32 = jnp.float32
BF16 = jnp.bfloat16

LANES = 128
MXU_COLS = 256
IN_PROJ_ROW_CHUNK = 512
EPILOGUE_ROWS = 32

FOX_HEAD_DIM = 128
RET_QK_DIM = 256
RET_V_DIM = 512
ROPE_BASE = 10000.0
LN_EPS = 1e-5
GN_EPS = 1e-6
QK_EPS = 1e-6
LOG2E = math.log2(math.e)

VMEM_LIMIT = 56 * 1024 * 1024

NT_DIMS = (((1,), (1,)), ((), ()))
TN_DIMS = (((0,), (0,)), ((), ()))


def _cparams(sem):
    return pltpu.CompilerParams(dimension_semantics=sem, vmem_limit_bytes=VMEM_LIMIT)


def _silu(g):
    return g * (1.0 / (1.0 + jnp.exp(-g)))


def _lagged_tile(t, n_tiles):
    return jnp.minimum(t, n_tiles - 1), jnp.maximum(t, 1) - 1


def _in_proj_body(x_ref, w_ref, o_ref, acc_ref, wb_ref, mixed_epilogue, *,
                  col_tiles, mixed_tiles, plain_tiles):
    t = pl.program_id(0)
    n_tiles = pl.num_programs(0) - 1
    _, done = _lagged_tile(t, n_tiles)
    jd = done % col_tiles
    tm, tn = o_ref.shape

    def step(mm_slot, done_slot, first_and_last):
        def run(do_matmul, epilogue):
            w_src = w_ref
            if do_matmul and w_ref.dtype != BF16:
                w = w_ref[...]
                if w.shape != wb_ref.shape:
                    w = w.T
                wb_ref[...] = w.astype(BF16)
                w_src = wb_ref
            for r in range(0, tm, IN_PROJ_ROW_CHUNK):
                rows = slice(r, r + IN_PROJ_ROW_CHUNK)
                if do_matmul:
                    acc_ref[mm_slot, rows, :] = jnp.dot(x_ref[rows, :], w_src[...],
                                                        preferred_element_type=F32)
                if epilogue is not None:
                    for rr in range(r, r + IN_PROJ_ROW_CHUNK, EPILOGUE_ROWS):
                        sub = slice(rr, rr + EPILOGUE_ROWS)
                        for c in range(tn // MXU_COLS):
                            cols = slice(c * MXU_COLS, (c + 1) * MXU_COLS)
                            col = c * MXU_COLS
                            for piece in epilogue(acc_ref[done_slot, sub, cols], sub):
                                o_ref[sub, col:col + piece.shape[1]] = piece.astype(o_ref.dtype)
                                col += piece.shape[1]

        running = (t > 0) & (t < n_tiles)
        kinds = [(jd < mixed_tiles, mixed_epilogue),
                 ((jd >= mixed_tiles) & (jd < plain_tiles), lambda acc, rows: [acc]),
                 (jd >= plain_tiles, lambda acc, rows: [_silu(acc)])]
        for cond, epilogue in kinds:
            pl.when(running & cond)(functools.partial(run, True, epilogue))
        if first_and_last:
            pl.when(t == 0)(functools.partial(run, True, None))
            pl.when(t == n_tiles)(functools.partial(run, False, kinds[-1][1]))

    @pl.when(t % 2 == 0)
    def _():
        step(0, 1, True)

    @pl.when(t % 2 == 1)
    def _():
        step(1, 0, False)


def _fox_in_kernel(x_ref, w_ref, g_ref, o_ref, acc_ref, wb_ref, **tiles):
    def rms(acc, rows):
        outs = []
        for hh in range(MXU_COLS // FOX_HEAD_DIM):
            a = acc[:, hh * FOX_HEAD_DIM:(hh + 1) * FOX_HEAD_DIM]
            ms = jnp.mean(a * a, axis=-1, keepdims=True)
            outs.append(a * lax.rsqrt(ms + QK_EPS) * g_ref[...])
        return outs

    _in_proj_body(x_ref, w_ref, o_ref, acc_ref, wb_ref, rms, **tiles)


def _in_proj_call(kern, name, x_bf, w_in, layer, aux, aux_specs, n, col_tiles, tm, tn,
                  w_cols_leading=False):
    m, d = x_bf.shape
    n_tiles = (m // tm) * col_tiles
    assert n_tiles % 2 == 0 and n == col_tiles * tn

    def mm_tile(t):
        return _lagged_tile(t, n_tiles)[0]

    def done_tile(t):
        return _lagged_tile(t, n_tiles)[1]

    return pl.pallas_call(
        kern,
        grid=(n_tiles + 1,),
        in_specs=[
            pl.BlockSpec((tm, d), lambda t: (mm_tile(t) // col_tiles, 0)),
            (pl.BlockSpec((None, tn, d), lambda t: (layer, mm_tile(t) % col_tiles, 0))
             if w_cols_leading else
             pl.BlockSpec((None, d, tn), lambda t: (layer, 0, mm_tile(t) % col_tiles))),
        ] + [pl.BlockSpec(shape, functools.partial(lambda t, f: f(done_tile(t)), f=f))
             for shape, f in aux_specs],
        out_specs=pl.BlockSpec((tm, tn), lambda t: (done_tile(t) // col_tiles,
                                                    done_tile(t) % col_tiles)),
        out_shape=jax.ShapeDtypeStruct((m, n), BF16),
        scratch_shapes=[
            pltpu.VMEM((2, tm, tn), F32),
            pltpu.VMEM((d, tn) if w_in.dtype != BF16 else (16, LANES), BF16),
        ],
        compiler_params=_cparams(("arbitrary",)),
        name=name,
    )(x_bf, w_in, *aux)


def _fox_in_proj(x_bf, w_in, layer, q_gain, k_gain, width, *, tm=2048, tn=512):
    col_tiles = 4 * width // tn
    per_region = width // tn
    gains = jnp.stack([q_gain * (FOX_HEAD_DIM ** -0.5 * LOG2E), k_gain]).reshape(2, 1, -1)
    kern = functools.partial(_fox_in_kernel, col_tiles=col_tiles, mixed_tiles=2 * per_region,
                             plain_tiles=3 * per_region)
    gain_spec = ((None, 1, FOX_HEAD_DIM),
                 lambda done: (jnp.minimum((done % col_tiles) // per_region, 1), 0, 0))
    return _in_proj_call(kern, "fox_in_proj", x_bf, w_in, layer, [gains], [gain_spec],
                         4 * width, col_tiles, tm, tn, w_cols_leading=True)


def _ret_in_kernel(x_ref, w_ref, cos_ref, sin_ref, o_ref, acc_ref, wb_ref, **tiles):
    half = RET_QK_DIM // 2

    def rope(acc, rows):
        cos = cos_ref[rows, :]
        sin = sin_ref[rows, :]
        t1 = acc[:, :half]
        t2 = acc[:, half:]
        return [t1 * cos - t2 * sin, t1 * sin + t2 * cos]

    _in_proj_body(x_ref, w_ref, o_ref, acc_ref, wb_ref, rope, **tiles)


def _ret_in_proj(x_bf, w_in, layer, cos2, sin2, seq, qk_width, v_width, *, tm=2048, tn=512):
    assert RET_QK_DIM == MXU_COLS
    n = w_in.shape[2]
    col_tiles = n // tn
    q_tiles = qk_width // tn
    half = RET_QK_DIM // 2
    pos_blocks = seq // tm
    kern = functools.partial(_ret_in_kernel, col_tiles=col_tiles, mixed_tiles=2 * q_tiles,
                             plain_tiles=(2 * qk_width + v_width) // tn)
    table_spec = ((None, tm, half),
                  lambda done: (jnp.minimum((done % col_tiles) // q_tiles, 1),
                                (done // col_tiles) % pos_blocks, 0))
    return _in_proj_call(kern, "ret_in_proj", x_bf, w_in, layer, [cos2, sin2],
                         [table_spec, table_spec], n, col_tiles, tm, tn)


N_SPLIT = 3


def _split3(v):
    hi = v.astype(BF16)
    r = v - hi.astype(F32)
    mid = r.astype(BF16)
    lo = (r - mid.astype(F32)).astype(BF16)
    return hi, mid, lo


def _fgate_kernel(x_ref, w_ref, b_ref, c3_ref, carry_ref):
    ts = x_ref.shape[0]
    n_heads = w_ref.shape[0]

    @pl.when(pl.program_id(1) == 0)
    def _():
        carry_ref[...] = jnp.zeros_like(carry_ref)

    z = lax.dot_general(x_ref[...], w_ref[...].astype(BF16), NT_DIMS,
                        preferred_element_type=F32) + b_ref[...]
    log_f = jnp.minimum(z, 0.0) - jnp.log1p(jnp.exp(-jnp.abs(z)))
    row = lax.broadcasted_iota(jnp.int32, (ts, ts), 0)
    col = lax.broadcasted_iota(jnp.int32, (ts, ts), 1)
    tri = jnp.where(col <= row, 1.0, 0.0).astype(BF16)
    cs = carry_ref[...]
    for part in _split3(log_f):
        cs = cs + jnp.dot(tri, part, preferred_element_type=F32)
    carry_ref[...] = cs[ts - 1:ts, :]
    src = lax.broadcasted_iota(jnp.int32, (n_heads, LANES), 0)
    dst = lax.broadcasted_iota(jnp.int32, (n_heads, LANES), 1)
    c3 = jnp.zeros((ts, LANES), F32)
    for t, part in enumerate(_split3(cs * LOG2E)):
        place = jnp.where(dst == t * n_heads + src, 1.0, 0.0).astype(BF16)
        c3 = c3 + jnp.dot(part, place, preferred_element_type=F32)
    c3_ref[...] = c3.astype(c3_ref.dtype)


def _fox_forget_cumsum(x_bf, w_in_t, layer, b_f, batch, seq, f_col0, *, ts=512):
    d = x_bf.shape[1]
    h = b_f.shape[0]
    ns = seq // ts
    assert f_col0 % h == 0 and w_in_t.shape[1] == f_col0 + h and N_SPLIT * h <= LANES
    return pl.pallas_call(
        _fgate_kernel,
        grid=(batch, ns),
        in_specs=[
            pl.BlockSpec((ts, d), lambda bi, si: (bi * ns + si, 0)),
            pl.BlockSpec((None, h, d), lambda bi, si: (layer, f_col0 // h, 0)),
            pl.BlockSpec((1, h), lambda bi, si: (0, 0)),
        ],
        out_specs=pl.BlockSpec((ts, LANES), lambda bi, si: (bi * ns + si, 0)),
        out_shape=jax.ShapeDtypeStruct((batch * seq, LANES), BF16),
        scratch_shapes=[pltpu.VMEM((1, h), F32)],
        compiler_params=_cparams(("parallel", "arbitrary")),
        name="fox_forget_cumsum",
    )(x_bf, w_in_t, b_f.reshape(1, -1))


def _fox_attn_kernel(q_ref, k_ref, v_ref, g_ref, c3_ref, o_ref,
                     qa_ref, kaug_ref, vaug_ref, m_ref, acc_ref, p_ref, alpha_ref,
                     *, blk, n_heads):
    h = pl.program_id(1)
    s_len = k_ref.shape[0]
    dh = FOX_HEAD_DIM
    n_blk = s_len // blk

    src = lax.broadcasted_iota(jnp.int32, (LANES, 2 * dh), 0)
    dst = lax.broadcasted_iota(jnp.int32, (LANES, 2 * dh), 1)
    to_q = (dst < N_SPLIT) & (src == dst * n_heads + h)
    dk = dst - (dh + N_SPLIT)
    to_k = (dk >= 0) & (dk < N_SPLIT) & (src == dk * n_heads + h)
    place = (jnp.where(to_q, 1.0, 0.0) - jnp.where(to_k, 1.0, 0.0)).astype(BF16)
    x = jnp.dot(c3_ref[...], place, preferred_element_type=F32)
    lane = lax.broadcasted_iota(jnp.int32, (s_len, dh), 1)
    qa_ref[:, :dh] = q_ref[...]
    qa_ref[:, dh:] = (x[:, :dh] + jnp.where((lane >= N_SPLIT) & (lane < 2 * N_SPLIT), 1.0, 0.0)
                      ).astype(BF16)
    kaug_ref[:, :dh] = k_ref[...]
    kaug_ref[:, dh:] = (x[:, dh:] + jnp.where(lane < N_SPLIT, 1.0, 0.0)).astype(BF16)
    vaug_ref[:, :dh] = v_ref[...]
    vaug_ref[:, dh:] = jnp.ones((s_len, dh), BF16)
    m_ref[...] = jnp.full(m_ref.shape, -jnp.inf, F32)
    acc_ref[...] = jnp.zeros_like(acc_ref)

    def qk_softmax(slot, qi, ki):
        rows = pl.ds(qi * blk, blk)
        s = lax.dot_general(qa_ref[rows, :], kaug_ref[pl.ds(ki * blk, blk), :], NT_DIMS,
                            preferred_element_type=F32)
        if ki == qi:
            row = lax.broadcasted_iota(jnp.int32, (blk, blk), 0)
            col = lax.broadcasted_iota(jnp.int32, (blk, blk), 1)
            s = jnp.where(col <= row, s, -jnp.inf)
        m_old = m_ref[rows, :]
        m_new = jnp.maximum(m_old, jnp.max(s, axis=-1, keepdims=True))
        alpha_ref[slot] = jnp.exp2(m_old - m_new)
        p_ref[slot] = jnp.exp2(s - jnp.concatenate([m_new] * (blk // LANES), axis=1)).astype(BF16)
        m_ref[rows, :] = m_new

    def pv_update(slot, qi, ki):
        rows = pl.ds(qi * blk, blk)
        pv = jnp.dot(p_ref[slot], vaug_ref[pl.ds(ki * blk, blk), :], preferred_element_type=F32)
        alpha = alpha_ref[slot]
        acc_ref[rows, :] = jnp.concatenate([alpha, alpha], axis=1) * acc_ref[rows, :] + pv
        if ki == qi:
            o_ref[rows, :] = (acc_ref[rows, :dh] / acc_ref[rows, dh:]
                              * g_ref[rows, :].astype(F32)).astype(o_ref.dtype)

    pairs = [(qi, ki) for qi in range(n_blk) for ki in range(qi + 1)]
    for t, pair in enumerate(pairs):
        qk_softmax(t % 2, *pair)
        if t > 0:
            pv_update((t - 1) % 2, *pairs[t - 1])
    pv_update((len(pairs) - 1) % 2, *pairs[-1])


def _fox_attention(proj, c3, batch, seq, *, blk=512):
    m, n = proj.shape
    width = n // 4
    heads = width // FOX_HEAD_DIM
    dh = FOX_HEAD_DIM
    kern = functools.partial(_fox_attn_kernel, blk=blk, n_heads=heads)
    return pl.pallas_call(
        kern,
        grid=(batch, heads),
        in_specs=[
            pl.BlockSpec((seq, dh), lambda b, h: (b, h)),
            pl.BlockSpec((seq, dh), lambda b, h: (b, heads + h)),
            pl.BlockSpec((seq, dh), lambda b, h: (b, 2 * heads + h)),
            pl.BlockSpec((seq, dh), lambda b, h: (b, 3 * heads + h)),
            pl.BlockSpec((seq, LANES), lambda b, h: (b, 0)),
        ],
        out_specs=pl.BlockSpec((seq, dh), lambda b, h: (b, h)),
        out_shape=jax.ShapeDtypeStruct((m, width), BF16),
        scratch_shapes=[
            pltpu.VMEM((seq, 2 * dh), BF16),
            pltpu.VMEM((seq, 2 * dh), BF16),
            pltpu.VMEM((seq, 2 * dh), BF16),
            pltpu.VMEM((seq, LANES), F32),
            pltpu.VMEM((seq, 2 * dh), F32),
            pltpu.VMEM((2, blk, blk), BF16),
            pltpu.VMEM((2, blk, LANES), F32),
        ],
        compiler_params=_cparams(("parallel", "parallel")),
        name="fox_attention",
    )(proj, proj, proj, proj, c3)


def _out_ln_kernel(y_ref, w_ref, h_ref, g_ref, b_ref, o_ref, obf_ref, *, alpha, row_group):
    for r in range(0, y_ref.shape[0], row_group):
        rows = slice(r, r + row_group)
        br = jnp.dot(y_ref[rows, :], w_ref[...], preferred_element_type=F32)
        z = alpha * h_ref[rows, :] + br
        mu = jnp.mean(z, axis=-1, keepdims=True)
        zc = z - mu
        var = jnp.mean(zc * zc, axis=-1, keepdims=True)
        out = zc * lax.rsqrt(var + LN_EPS) * g_ref[...] + b_ref[...]
        o_ref[rows, :] = out
        obf_ref[rows, :] = out.astype(obf_ref.dtype)


def _out_proj_ln(y_bf, w_bf, layer, h, gain, bias, alpha, *, tm=512, row_group=256):
    m, k = y_bf.shape
    d = w_bf.shape[2]
    kern = functools.partial(_out_ln_kernel, alpha=alpha, row_group=row_group)
    return pl.pallas_call(
        kern,
        grid=(m // tm,),
        in_specs=[
            pl.BlockSpec((tm, k), lambda i: (i, 0)),
            pl.BlockSpec((None, k, d), lambda i: (layer, 0, 0), pipeline_mode=pl.Buffered(1)),
            pl.BlockSpec((tm, d), lambda i: (i, 0)),
            pl.BlockSpec((1, d), lambda i: (0, 0)),
            pl.BlockSpec((1, d), lambda i: (0, 0)),
        ],
        out_specs=[pl.BlockSpec((tm, d), lambda i: (i, 0)),
                   pl.BlockSpec((tm, d), lambda i: (i, 0))],
        out_shape=[jax.ShapeDtypeStruct((m, d), F32), jax.ShapeDtypeStruct((m, d), BF16)],
        compiler_params=_cparams(("parallel",)),
        name="out_proj_ln",
    )(y_bf, w_bf, h, gain.reshape(1, -1), bias.reshape(1, -1))


def _retention_kernel(lg_ref, q_ref, k_ref, v_ref, g_ref, o_ref, r_ref, *, chunk):
    h = pl.program_id(1)
    n = pl.program_id(2)

    @pl.when(n == 0)
    def _():
        r_ref[...] = jnp.zeros_like(r_ref)

    lg = lg_ref[h]
    row = lax.broadcasted_iota(jnp.int32, (chunk, chunk), 0)
    col = lax.broadcasted_iota(jnp.int32, (chunk, chunk), 1)
    diff = (row - col).astype(F32)
    intra = jnp.where(diff >= 0, jnp.exp(jnp.maximum(diff, 0.0) * lg), 0.0)
    pos = lax.broadcasted_iota(jnp.int32, (chunk, 1), 0).astype(F32)
    q_decay = jnp.exp((pos + 1.0) * lg)
    k_decay = jnp.exp((chunk - 1.0 - pos) * lg)
    c_decay = jnp.exp(jnp.full((1, 1), chunk, F32) * lg)

    r = r_ref[...]
    for ci in range(q_ref.shape[0] // chunk):
        rows = slice(ci * chunk, (ci + 1) * chunk)
        q = q_ref[rows, :]
        k = k_ref[rows, :]
        v = v_ref[rows, :]
        inner = lax.dot_general(q, k, NT_DIMS, preferred_element_type=F32) * intra
        o = (jnp.dot(inner.astype(BF16), v, preferred_element_type=F32)
             + jnp.dot(q, r.astype(BF16), preferred_element_type=F32) * q_decay)
        kd = (k.astype(F32) * k_decay).astype(BF16)
        r = r * c_decay + lax.dot_general(kd, v, TN_DIMS, preferred_element_type=F32)

        mu = jnp.mean(o, axis=-1, keepdims=True)
        oc = o - mu
        var = jnp.mean(oc * oc, axis=-1, keepdims=True)
        o_ref[rows, :] = (oc * lax.rsqrt(var + GN_EPS)
                          * g_ref[rows, :].astype(F32)).astype(o_ref.dtype)
    r_ref[...] = r


def _retention(proj, log_gamma, batch, seq, heads, *, chunk=256, chunks_per_step=16):
    m = proj.shape[0]
    ts = chunk * chunks_per_step
    ns = seq // ts
    dk, dv = RET_QK_DIM, RET_V_DIM
    v_off = 2 * heads * dk // dv
    g_off = v_off + heads
    kern = functools.partial(_retention_kernel, chunk=chunk)
    return pl.pallas_call(
        kern,
        grid_spec=pltpu.PrefetchScalarGridSpec(
            num_scalar_prefetch=1,
            grid=(batch, heads, ns),
            in_specs=[
                pl.BlockSpec((ts, dk), lambda b, h, n, lg: (b * ns + n, h)),
                pl.BlockSpec((ts, dk), lambda b, h, n, lg: (b * ns + n, heads + h)),
                pl.BlockSpec((ts, dv), lambda b, h, n, lg: (b * ns + n, v_off + h)),
                pl.BlockSpec((ts, dv), lambda b, h, n, lg: (b * ns + n, g_off + h)),
            ],
            out_specs=pl.BlockSpec((ts, dv), lambda b, h, n, lg: (b * ns + n, h)),
            scratch_shapes=[pltpu.VMEM((dk, dv), F32)],
        ),
        out_shape=jax.ShapeDtypeStruct((m, heads * dv), BF16),
        compiler_params=_cparams(("parallel", "parallel", "arbitrary")),
        name="retention",
    )(log_gamma, proj, proj, proj, proj)


def kernel(x, fox_w_in, fox_b_f, fox_q_gain, fox_k_gain, fox_w_out, ret_w_in, ret_w_out,
           ln_gain, ln_bias):
    batch, seq, d_model = x.shape
    depth = ln_gain.shape[0]
    m = batch * seq
    alpha = (2.0 * depth) ** 0.25

    fox_width = fox_w_out.shape[1]
    ret_v_width = ret_w_out.shape[1]
    ret_heads = ret_v_width // RET_V_DIM
    ret_qk_width = ret_heads * RET_QK_DIM

    inv_freq = ROPE_BASE ** (-jnp.arange(0, RET_QK_DIM, 2, dtype=F32) / RET_QK_DIM)
    ang = jnp.arange(seq, dtype=F32)[:, None] * inv_freq[None, :]
    k_scale = RET_QK_DIM ** -0.5
    cos2 = jnp.stack([jnp.cos(ang), jnp.cos(ang) * k_scale])
    sin2 = jnp.stack([jnp.sin(ang), jnp.sin(ang) * k_scale])
    log_gamma = jnp.log1p(-jnp.exp2(-5.0 - jnp.arange(ret_heads, dtype=F32)))

    fox_w_in_t = jnp.swapaxes(fox_w_in, 1, 2)
    fox_w_in_bf = fox_w_in.astype(BF16)
    fox_w_out_bf = fox_w_out.astype(BF16)
    ret_w_out_bf = ret_w_out.astype(BF16)
    h = x.reshape(m, d_model)
    h_bf = h.astype(BF16)
    for i in range(depth):
        j = i // 2
        if i % 2 == 0:
            proj = _fox_in_proj(h_bf, fox_w_in_t, j, fox_q_gain[j], fox_k_gain[j], fox_width)
            c3 = _fox_forget_cumsum(h_bf, fox_w_in_t, j, fox_b_f[j], batch, seq, 4 * fox_width)
            y = _fox_attention(proj, c3, batch, seq)
            w_out = fox_w_out_bf
        else:
            proj = _ret_in_proj(h_bf, ret_w_in, j, cos2, sin2, seq, ret_qk_width, ret_v_width)
            y = _retention(proj, log_gamma, batch, seq, ret_heads)
            w_out = ret_w_out_bf
        h, h_bf = _out_proj_ln(y, w_out, j, h, ln_gain[i], ln_bias[i], alpha)
    return h.reshape(batch, seq, d_model)
```

```python
import functools
import math

import jax
import jax.numpy as jnp
from jax import lax
from jax.experimental import pallas as pl
from jax.experimental.pallas import tpu as pltpu

F32 = jnp.float32
BF16 = jnp.bfloat16

LANES = 128
MXU_COLS = 256
IN_PROJ_ROW_CHUNK = 512
EPILOGUE_ROWS = 32

FOX_HEAD_DIM = 128
RET_QK_DIM = 256
RET_V_DIM = 512
ROPE_BASE = 10000.0
LN_EPS = 1e-5
GN_EPS = 1e-6
QK_EPS = 1e-6
LOG2E = math.log2(math.e)

VMEM_LIMIT = 56 * 1024 * 1024

NT_DIMS = (((1,), (1,)), ((), ()))
TN_DIMS = (((0,), (0,)), ((), ()))


def _cparams(sem):
    return pltpu.CompilerParams(dimension_semantics=sem, vmem_limit_bytes=VMEM_LIMIT)


def _silu(g):
    return g * (1.0 / (1.0 + jnp.exp(-g)))


def _lagged_tile(t, n_tiles):
    return jnp.minimum(t, n_tiles - 1), jnp.maximum(t, 1) - 1


def _in_proj_body(x_ref, w_ref, o_ref, acc_ref, wb_ref, mixed_epilogue, *,
                  col_tiles, mixed_tiles, plain_tiles):
    t = pl.program_id(0)
    n_tiles = pl.num_programs(0) - 1
    _, done = _lagged_tile(t, n_tiles)
    jd = done % col_tiles
    tm, tn = o_ref.shape

    def step(mm_slot, done_slot, first_and_last):
        def run(do_matmul, epilogue):
            w_src = w_ref
            if do_matmul and w_ref.dtype != BF16:
                w = w_ref[...]
                if w.shape != wb_ref.shape:
                    w = w.T
                wb_ref[...] = w.astype(BF16)
                w_src = wb_ref
            for r in range(0, tm, IN_PROJ_ROW_CHUNK):
                rows = slice(r, r + IN_PROJ_ROW_CHUNK)
                if do_matmul:
                    acc_ref[mm_slot, rows, :] = jnp.dot(x_ref[rows, :], w_src[...],
                                                        preferred_element_type=F32)
                if epilogue is not None:
                    for rr in range(r, r + IN_PROJ_ROW_CHUNK, EPILOGUE_ROWS):
                        sub = slice(rr, rr + EPILOGUE_ROWS)
                        for c in range(tn // MXU_COLS):
                            cols = slice(c * MXU_COLS, (c + 1) * MXU_COLS)
                            col = c * MXU_COLS
                            for piece in epilogue(acc_ref[done_slot, sub, cols], sub):
                                o_ref[sub, col:col + piece.shape[1]] = piece.astype(o_ref.dtype)
                                col += piece.shape[1]

        running = (t > 0) & (t < n_tiles)
        kinds = [(jd < mixed_tiles, mixed_epilogue),
                 ((jd >= mixed_tiles) & (jd < plain_tiles), lambda acc, rows: [acc]),
                 (jd >= plain_tiles, lambda acc, rows: [_silu(acc)])]
        for cond, epilogue in kinds:
            pl.when(running & cond)(functools.partial(run, True, epilogue))
        if first_and_last:
            pl.when(t == 0)(functools.partial(run, True, None))
            pl.when(t == n_tiles)(functools.partial(run, False, kinds[-1][1]))

    @pl.when(t % 2 == 0)
    def _():
        step(0, 1, True)

    @pl.when(t % 2 == 1)
    def _():
        step(1, 0, False)


def _fox_in_kernel(x_ref, w_ref, g_ref, o_ref, acc_ref, wb_ref, **tiles):
    def rms(acc, rows):
        outs = []
        for hh in range(MXU_COLS // FOX_HEAD_DIM):
            a = acc[:, hh * FOX_HEAD_DIM:(hh + 1) * FOX_HEAD_DIM]
            ms = jnp.mean(a * a, axis=-1, keepdims=True)
            outs.append(a * lax.rsqrt(ms + QK_EPS) * g_ref[...])
        return outs

    _in_proj_body(x_ref, w_ref, o_ref, acc_ref, wb_ref, rms, **tiles)


def _in_proj_call(kern, name, x_bf, w_in, layer, aux, aux_specs, n, col_tiles, tm, tn,
                  w_cols_leading=False):
    m, d = x_bf.shape
    n_tiles = (m // tm) * col_tiles
    assert n_tiles % 2 == 0 and n == col_tiles * tn

    def mm_tile(t):
        return _lagged_tile(t, n_tiles)[0]

    def done_tile(t):
        return _lagged_tile(t, n_tiles)[1]

    return pl.pallas_call(
        kern,
        grid=(n_tiles + 1,),
        in_specs=[
            pl.BlockSpec((tm, d), lambda t: (mm_tile(t) // col_tiles, 0)),
            (pl.BlockSpec((None, tn, d), lambda t: (layer, mm_tile(t) % col_tiles, 0))
             if w_cols_leading else
             pl.BlockSpec((None, d, tn), lambda t: (layer, 0, mm_tile(t) % col_tiles))),
        ] + [pl.BlockSpec(shape, functools.partial(lambda t, f: f(done_tile(t)), f=f))
             for shape, f in aux_specs],
        out_specs=pl.BlockSpec((tm, tn), lambda t: (done_tile(t) // col_tiles,
                                                    done_tile(t) % col_tiles)),
        out_shape=jax.ShapeDtypeStruct((m, n), BF16),
        scratch_shapes=[
            pltpu.VMEM((2, tm, tn), F32),
            pltpu.VMEM((d, tn) if w_in.dtype != BF16 else (16, LANES), BF16),
        ],
        compiler_params=_cparams(("arbitrary",)),
        name=name,
    )(x_bf, w_in, *aux)


def _fox_in_proj(x_bf, w_in_t, layer, q_gain, k_gain, width, *, tm=2048, tn=512):
    col_tiles = 4 * width // tn
    per_region = width // tn
    gains = jnp.stack([q_gain * (FOX_HEAD_DIM ** -0.5 * LOG2E), k_gain]).reshape(2, 1, -1)
    kern = functools.partial(_fox_in_kernel, col_tiles=col_tiles, mixed_tiles=2 * per_region,
                             plain_tiles=3 * per_region)
    gain_spec = ((None, 1, FOX_HEAD_DIM),
                 lambda done: (jnp.minimum((done % col_tiles) // per_region, 1), 0, 0))
    return _in_proj_call(kern, "fox_in_proj", x_bf, w_in_t, layer, [gains], [gain_spec],
                         4 * width, col_tiles, tm, tn, w_cols_leading=True)


def _ret_in_kernel(x_ref, w_ref, cos_ref, sin_ref, o_ref, acc_ref, wb_ref, **tiles):
    half = RET_QK_DIM // 2

    def rope(acc, rows):
        cos = cos_ref[rows, :]
        sin = sin_ref[rows, :]
        t1 = acc[:, :half]
        t2 = acc[:, half:]
        return [t1 * cos - t2 * sin, t1 * sin + t2 * cos]

    _in_proj_body(x_ref, w_ref, o_ref, acc_ref, wb_ref, rope, **tiles)


def _ret_in_proj(x_bf, w_in, layer, cos2, sin2, seq, qk_width, v_width, *, tm=2048, tn=512):
    assert RET_QK_DIM == MXU_COLS
    n = w_in.shape[2]
    col_tiles = n // tn
    q_tiles = qk_width // tn
    half = RET_QK_DIM // 2
    pos_blocks = seq // tm
    kern = functools.partial(_ret_in_kernel, col_tiles=col_tiles, mixed_tiles=2 * q_tiles,
                             plain_tiles=(2 * qk_width + v_width) // tn)
    table_spec = ((None, tm, half),
                  lambda done: (jnp.minimum((done % col_tiles) // q_tiles, 1),
                                (done // col_tiles) % pos_blocks, 0))
    return _in_proj_call(kern, "ret_in_proj", x_bf, w_in, layer, [cos2, sin2],
                         [table_spec, table_spec], n, col_tiles, tm, tn)


N_SPLIT = 3


def _split3(v):
    hi = v.astype(BF16)
    r = v - hi.astype(F32)
    mid = r.astype(BF16)
    lo = (r - mid.astype(F32)).astype(BF16)
    return hi, mid, lo


def _fgate_kernel(x_ref, w_ref, b_ref, c3_ref, carry_ref):
    ts = x_ref.shape[0]
    n_heads = w_ref.shape[0]

    @pl.when(pl.program_id(1) == 0)
    def _():
        carry_ref[...] = jnp.zeros_like(carry_ref)

    z = lax.dot_general(x_ref[...], w_ref[...].astype(BF16), NT_DIMS,
                        preferred_element_type=F32) + b_ref[...]
    log_f = jnp.minimum(z, 0.0) - jnp.log1p(jnp.exp(-jnp.abs(z)))
    row = lax.broadcasted_iota(jnp.int32, (ts, ts), 0)
    col = lax.broadcasted_iota(jnp.int32, (ts, ts), 1)
    tri = jnp.where(col <= row, 1.0, 0.0).astype(BF16)
    cs = carry_ref[...]
    for part in _split3(log_f):
        cs = cs + jnp.dot(tri, part, preferred_element_type=F32)
    carry_ref[...] = cs[ts - 1:ts, :]
    src = lax.broadcasted_iota(jnp.int32, (n_heads, LANES), 0)
    dst = lax.broadcasted_iota(jnp.int32, (n_heads, LANES), 1)
    c3 = jnp.zeros((ts, LANES), F32)
    for t, part in enumerate(_split3(cs * LOG2E)):
        place = jnp.where(dst == t * n_heads + src, 1.0, 0.0).astype(BF16)
        c3 = c3 + jnp.dot(part, place, preferred_element_type=F32)
    c3_ref[...] = c3.astype(c3_ref.dtype)


def _fox_forget_cumsum(x_bf, w_in_t, layer, b_f, batch, seq, f_col0, *, ts=512):
    d = x_bf.shape[1]
    h = b_f.shape[0]
    ns = seq // ts
    assert f_col0 % h == 0 and w_in_t.shape[1] == f_col0 + h and N_SPLIT * h <= LANES
    return pl.pallas_call(
        _fgate_kernel,
        grid=(batch, ns),
        in_specs=[
            pl.BlockSpec((ts, d), lambda bi, si: (bi * ns + si, 0)),
            pl.BlockSpec((None, h, d), lambda bi, si: (layer, f_col0 // h, 0)),
            pl.BlockSpec((1, h), lambda bi, si: (0, 0)),
        ],
        out_specs=pl.BlockSpec((ts, LANES), lambda bi, si: (bi * ns + si, 0)),
        out_shape=jax.ShapeDtypeStruct((batch * seq, LANES), BF16),
        scratch_shapes=[pltpu.VMEM((1, h), F32)],
        compiler_params=_cparams(("parallel", "arbitrary")),
        name="fox_forget_cumsum",
    )(x_bf, w_in_t, b_f.reshape(1, -1))


def _fox_attn_kernel(q_ref, k_ref, v_ref, g_ref, c3_ref, o_ref,
                     qa_ref, kaug_ref, vaug_ref, m_ref, acc_ref, p_ref, alpha_ref,
                     *, blk, n_heads):
    h = pl.program_id(1)
    s_len = k_ref.shape[0]
    dh = FOX_HEAD_DIM
    n_blk = s_len // blk

    src = lax.broadcasted_iota(jnp.int32, (LANES, 2 * dh), 0)
    dst = lax.broadcasted_iota(jnp.int32, (LANES, 2 * dh), 1)
    to_q = (dst < N_SPLIT) & (src == dst * n_heads + h)
    dk = dst - (dh + N_SPLIT)
    to_k = (dk >= 0) & (dk < N_SPLIT) & (src == dk * n_heads + h)
    place = (jnp.where(to_q, 1.0, 0.0) - jnp.where(to_k, 1.0, 0.0)).astype(BF16)
    x = jnp.dot(c3_ref[...], place, preferred_element_type=F32)
    lane = lax.broadcasted_iota(jnp.int32, (s_len, dh), 1)
    qa_ref[:, :dh] = q_ref[...]
    qa_ref[:, dh:] = (x[:, :dh] + jnp.where((lane >= N_SPLIT) & (lane < 2 * N_SPLIT), 1.0, 0.0)
                      ).astype(BF16)
    kaug_ref[:, :dh] = k_ref[...]
    kaug_ref[:, dh:] = (x[:, dh:] + jnp.where(lane < N_SPLIT, 1.0, 0.0)).astype(BF16)
    vaug_ref[:, :dh] = v_ref[...]
    vaug_ref[:, dh:] = jnp.ones((s_len, dh), BF16)
    m_ref[...] = jnp.full(m_ref.shape, -jnp.inf, F32)
    acc_ref[...] = jnp.zeros_like(acc_ref)

    def qk_softmax(slot, r0, nr, k0, nk, last):
        rows = pl.ds(r0, nr)
        s = lax.dot_general(qa_ref[rows, :], kaug_ref[pl.ds(k0, nk), :], NT_DIMS,
                            preferred_element_type=F32)
        if last:
            row = lax.broadcasted_iota(jnp.int32, (nr, nk), 0) + r0
            col = lax.broadcasted_iota(jnp.int32, (nr, nk), 1) + k0
            s = jnp.where(col <= row, s, -jnp.inf)
        m_old = m_ref[rows, :]
        m_new = jnp.maximum(m_old, jnp.max(s, axis=-1, keepdims=True))
        alpha_ref[slot, :nr, :] = jnp.exp2(m_old - m_new)
        p_ref[slot, :nr, :nk] = jnp.exp2(
            s - jnp.concatenate([m_new] * (nk // LANES), axis=1)).astype(BF16)
        m_ref[rows, :] = m_new

    def pv_update(slot, r0, nr, k0, nk, last):
        rows = pl.ds(r0, nr)
        pv = jnp.dot(p_ref[slot, :nr, :nk], vaug_ref[pl.ds(k0, nk), :],
                     preferred_element_type=F32)
        alpha = alpha_ref[slot, :nr, :]
        acc_ref[rows, :] = jnp.concatenate([alpha, alpha], axis=1) * acc_ref[rows, :] + pv
        if last:
            o_ref[rows, :] = (acc_ref[rows, :dh] / acc_ref[rows, dh:]
                              * g_ref[rows, :].astype(F32)).astype(o_ref.dtype)

    half = blk // 2
    tasks = []
    for qi in range(n_blk):
        q0 = qi * blk
        tasks += [(q0, blk, ki * blk, blk, False) for ki in range(qi)]
        tasks += [(q0, half, q0, half, True), (q0 + half, half, q0, blk, True)]
    for t, task in enumerate(tasks):
        qk_softmax(t % 2, *task)
        if t > 0:
            pv_update((t - 1) % 2, *tasks[t - 1])
    pv_update((len(tasks) - 1) % 2, *tasks[-1])


def _fox_attention(proj, c3, batch, seq, *, blk=512):
    m, n = proj.shape
    width = n // 4
    heads = width // FOX_HEAD_DIM
    dh = FOX_HEAD_DIM
    kern = functools.partial(_fox_attn_kernel, blk=blk, n_heads=heads)
    return pl.pallas_call(
        kern,
        grid=(batch, heads),
        in_specs=[
            pl.BlockSpec((seq, dh), lambda b, h: (b, h)),
            pl.BlockSpec((seq, dh), lambda b, h: (b, heads + h)),
            pl.BlockSpec((seq, dh), lambda b, h: (b, 2 * heads + h)),
            pl.BlockSpec((seq, dh), lambda b, h: (b, 3 * heads + h)),
            pl.BlockSpec((seq, LANES), lambda b, h: (b, 0)),
        ],
        out_specs=pl.BlockSpec((seq, dh), lambda b, h: (b, h)),
        out_shape=jax.ShapeDtypeStruct((m, width), BF16),
        scratch_shapes=[
            pltpu.VMEM((seq, 2 * dh), BF16),
            pltpu.VMEM((seq, 2 * dh), BF16),
            pltpu.VMEM((seq, 2 * dh), BF16),
            pltpu.VMEM((seq, LANES), F32),
            pltpu.VMEM((seq, 2 * dh), F32),
            pltpu.VMEM((2, blk, blk), BF16),
            pltpu.VMEM((2, blk, LANES), F32),
        ],
        compiler_params=_cparams(("parallel", "parallel")),
        name="fox_attention",
    )(proj, proj, proj, proj, c3)


def _out_ln_kernel(y_ref, w_ref, h_ref, g_ref, b_ref, o_ref, obf_ref, *, alpha, row_group):
    for r in range(0, y_ref.shape[0], row_group):
        rows = slice(r, r + row_group)
        br = jnp.dot(y_ref[rows, :], w_ref[...], preferred_element_type=F32)
        z = alpha * h_ref[rows, :] + br
        mu = jnp.mean(z, axis=-1, keepdims=True)
        zc = z - mu
        var = jnp.mean(zc * zc, axis=-1, keepdims=True)
        out = zc * lax.rsqrt(var + LN_EPS) * g_ref[...] + b_ref[...]
        o_ref[rows, :] = out
        obf_ref[rows, :] = out.astype(obf_ref.dtype)


def _out_proj_ln(y_bf, w_bf, layer, h, gain, bias, alpha, *, tm=512, row_group=256):
    m, k = y_bf.shape
    d = w_bf.shape[2]
    kern = functools.partial(_out_ln_kernel, alpha=alpha, row_group=row_group)
    return pl.pallas_call(
        kern,
        grid=(m // tm,),
        in_specs=[
            pl.BlockSpec((tm, k), lambda i: (i, 0)),
            pl.BlockSpec((None, k, d), lambda i: (layer, 0, 0), pipeline_mode=pl.Buffered(1)),
            pl.BlockSpec((tm, d), lambda i: (i, 0)),
            pl.BlockSpec((1, d), lambda i: (0, 0)),
            pl.BlockSpec((1, d), lambda i: (0, 0)),
        ],
        out_specs=[pl.BlockSpec((tm, d), lambda i: (i, 0)),
                   pl.BlockSpec((tm, d), lambda i: (i, 0))],
        out_shape=[jax.ShapeDtypeStruct((m, d), F32), jax.ShapeDtypeStruct((m, d), BF16)],
        compiler_params=_cparams(("parallel",)),
        name="out_proj_ln",
    )(y_bf, w_bf, h, gain.reshape(1, -1), bias.reshape(1, -1))


def _retention_kernel(lg_ref, q_ref, k_ref, v_ref, g_ref, o_ref, r_ref, *, chunk):
    h = pl.program_id(1)
    n = pl.program_id(2)

    @pl.when(n == 0)
    def _():
        r_ref[...] = jnp.zeros_like(r_ref)

    lg = lg_ref[h]
    row = lax.broadcasted_iota(jnp.int32, (chunk, chunk), 0)
    col = lax.broadcasted_iota(jnp.int32, (chunk, chunk), 1)
    diff = (row - col).astype(F32)
    intra = jnp.where(diff >= 0, jnp.exp(jnp.maximum(diff, 0.0) * lg), 0.0)
    pos = lax.broadcasted_iota(jnp.int32, (chunk, 1), 0).astype(F32)
    q_decay = jnp.exp((pos + 1.0) * lg)
    k_decay = jnp.exp((chunk - 1.0 - pos) * lg)
    c_decay = jnp.exp(jnp.full((1, 1), chunk, F32) * lg)

    r = r_ref[...]
    for ci in range(q_ref.shape[0] // chunk):
        rows = slice(ci * chunk, (ci + 1) * chunk)
        q = q_ref[rows, :]
        k = k_ref[rows, :]
        v = v_ref[rows, :]
        inner = lax.dot_general(q, k, NT_DIMS, preferred_element_type=F32) * intra
        o = (jnp.dot(inner.astype(BF16), v, preferred_element_type=F32)
             + jnp.dot(q, r.astype(BF16), preferred_element_type=F32) * q_decay)
        kd = (k.astype(F32) * k_decay).astype(BF16)
        r = r * c_decay + lax.dot_general(kd, v, TN_DIMS, preferred_element_type=F32)

        mu = jnp.mean(o, axis=-1, keepdims=True)
        oc = o - mu
        var = jnp.mean(oc * oc, axis=-1, keepdims=True)
        o_ref[rows, :] = (oc * lax.rsqrt(var + GN_EPS)
                          * g_ref[rows, :].astype(F32)).astype(o_ref.dtype)
    r_ref[...] = r


def _retention(proj, log_gamma, batch, seq, heads, *, chunk=256, chunks_per_step=16):
    m = proj.shape[0]
    ts = chunk * chunks_per_step
    ns = seq // ts
    dk, dv = RET_QK_DIM, RET_V_DIM
    v_off = 2 * heads * dk // dv
    g_off = v_off + heads
    kern = functools.partial(_retention_kernel, chunk=chunk)
    return pl.pallas_call(
        kern,
        grid_spec=pltpu.PrefetchScalarGridSpec(
            num_scalar_prefetch=1,
            grid=(batch, heads, ns),
            in_specs=[
                pl.BlockSpec((ts, dk), lambda b, h, n, lg: (b * ns + n, h)),
                pl.BlockSpec((ts, dk), lambda b, h, n, lg: (b * ns + n, heads + h)),
                pl.BlockSpec((ts, dv), lambda b, h, n, lg: (b * ns + n, v_off + h)),
                pl.BlockSpec((ts, dv), lambda b, h, n, lg: (b * ns + n, g_off + h)),
            ],
            out_specs=pl.BlockSpec((ts, dv), lambda b, h, n, lg: (b * ns + n, h)),
            scratch_shapes=[pltpu.VMEM((dk, dv), F32)],
        ),
        out_shape=jax.ShapeDtypeStruct((m, heads * dv), BF16),
        compiler_params=_cparams(("parallel", "parallel", "arbitrary")),
        name="retention",
    )(log_gamma, proj, proj, proj, proj)


def kernel(x, fox_w_in, fox_b_f, fox_q_gain, fox_k_gain, fox_w_out, ret_w_in, ret_w_out,
           ln_gain, ln_bias):
    batch, seq, d_model = x.shape
    depth = ln_gain.shape[0]
    m = batch * seq
    alpha = (2.0 * depth) ** 0.25

    fox_width = fox_w_out.shape[1]
    ret_v_width = ret_w_out.shape[1]
    ret_heads = ret_v_width // RET_V_DIM
    ret_qk_width = ret_heads * RET_QK_DIM

    inv_freq = ROPE_BASE ** (-jnp.arange(0, RET_QK_DIM, 2, dtype=F32) / RET_QK_DIM)
    ang = jnp.arange(seq, dtype=F32)[:, None] * inv_freq[None, :]
    k_scale = RET_QK_DIM ** -0.5
    cos2 = jnp.stack([jnp.cos(ang), jnp.cos(ang) * k_scale])
    sin2 = jnp.stack([jnp.sin(ang), jnp.sin(ang) * k_scale])
    log_gamma = jnp.log1p(-jnp.exp2(-5.0 - jnp.arange(ret_heads, dtype=F32)))

    fox_w_in_t = jnp.swapaxes(fox_w_in, 1, 2)
    fox_w_out_bf = fox_w_out.astype(BF16)
    ret_w_out_bf = ret_w_out.astype(BF16)
    h = x.reshape(m, d_model)
    h_bf = h.astype(BF16)
    for i in range(depth):
        j = i // 2
        if i % 2 == 0:
            proj = _fox_in_proj(h_bf, fox_w_in_t, j, fox_q_gain[j], fox_k_gain[j], fox_width)
            c3 = _fox_forget_cumsum(h_bf, fox_w_in_t, j, fox_b_f[j], batch, seq, 4 * fox_width)
            y = _fox_attention(proj, c3, batch, seq)
            w_out = fox_w_out_bf
        else:
            proj = _ret_in_proj(h_bf, ret_w_in, j, cos2, sin2, seq, ret_qk_width, ret_v_width)
            y = _retention(proj, log_gamma, batch, seq, ret_heads)
            w_out = ret_w_out_bf
        h, h_bf = _out_proj_ln(y, w_out, j, h, ln_gain[i], ln_bias[i], alpha)
    return h.reshape(batch, seq, d_model)
```

```python
import functools
import math

import jax
import jax.numpy as jnp
from jax import lax
from jax.experimental import pallas as pl
from jax.experimental.pallas import tpu as pltpu

F32 = jnp.float32
BF16 = jnp.bfloat16

LANES = 128
MXU_COLS = 256
IN_PROJ_ROW_CHUNK = 512
EPILOGUE_ROWS = 32

FOX_HEAD_DIM = 128
RET_QK_DIM = 256
RET_V_DIM = 512
ROPE_BASE = 10000.0
LN_EPS = 1e-5
GN_EPS = 1e-6
QK_EPS = 1e-6
LOG2E = math.log2(math.e)

VMEM_LIMIT = 56 * 1024 * 1024

NT_DIMS = (((1,), (1,)), ((), ()))
TN_DIMS = (((0,), (0,)), ((), ()))


def _cparams(sem):
    return pltpu.CompilerParams(dimension_semantics=sem, vmem_limit_bytes=VMEM_LIMIT)


def _silu(g):
    return g * (1.0 / (1.0 + jnp.exp(-g)))


def _lagged_tile(t, n_tiles):
    return jnp.minimum(t, n_tiles - 1), jnp.maximum(t, 1) - 1


def _in_proj_body(x_ref, w_ref, o_ref, acc_ref, wb_ref, mixed_epilogue, *,
                  col_tiles, mixed_tiles, plain_tiles):
    t = pl.program_id(0)
    n_tiles = pl.num_programs(0) - 1
    _, done = _lagged_tile(t, n_tiles)
    jd = done % col_tiles
    tm, tn = o_ref.shape

    def step(mm_slot, done_slot, first_and_last):
        def run(do_matmul, epilogue):
            w_src = w_ref
            if do_matmul and w_ref.dtype != BF16:
                w = w_ref[...]
                if w.shape != wb_ref.shape:
                    w = w.T
                wb_ref[...] = w.astype(BF16)
                w_src = wb_ref
            for r in range(0, tm, IN_PROJ_ROW_CHUNK):
                rows = slice(r, r + IN_PROJ_ROW_CHUNK)
                if do_matmul:
                    acc_ref[mm_slot, rows, :] = jnp.dot(x_ref[rows, :], w_src[...],
                                                        preferred_element_type=F32)
                if epilogue is not None:
                    for rr in range(r, r + IN_PROJ_ROW_CHUNK, EPILOGUE_ROWS):
                        sub = slice(rr, rr + EPILOGUE_ROWS)
                        for c in range(tn // MXU_COLS):
                            cols = slice(c * MXU_COLS, (c + 1) * MXU_COLS)
                            col = c * MXU_COLS
                            for piece in epilogue(acc_ref[done_slot, sub, cols], sub):
                                o_ref[sub, col:col + piece.shape[1]] = piece.astype(o_ref.dtype)
                                col += piece.shape[1]

        running = (t > 0) & (t < n_tiles)
        kinds = [(jd < mixed_tiles, mixed_epilogue),
                 ((jd >= mixed_tiles) & (jd < plain_tiles), lambda acc, rows: [acc]),
                 (jd >= plain_tiles, lambda acc, rows: [_silu(acc)])]
        for cond, epilogue in kinds:
            pl.when(running & cond)(functools.partial(run, True, epilogue))
        if first_and_last:
            pl.when(t == 0)(functools.partial(run, True, None))
            pl.when(t == n_tiles)(functools.partial(run, False, kinds[-1][1]))

    @pl.when(t % 2 == 0)
    def _():
        step(0, 1, True)

    @pl.when(t % 2 == 1)
    def _():
        step(1, 0, False)


def _fox_in_kernel(x_ref, w_ref, g_ref, o_ref, acc_ref, wb_ref, **tiles):
    def rms(acc, rows):
        outs = []
        for hh in range(MXU_COLS // FOX_HEAD_DIM):
            a = acc[:, hh * FOX_HEAD_DIM:(hh + 1) * FOX_HEAD_DIM]
            ms = jnp.mean(a * a, axis=-1, keepdims=True)
            outs.append(a * lax.rsqrt(ms + QK_EPS) * g_ref[...])
        return outs

    _in_proj_body(x_ref, w_ref, o_ref, acc_ref, wb_ref, rms, **tiles)


def _in_proj_call(kern, name, x_bf, w_in, layer, aux, aux_specs, n, col_tiles, tm, tn,
                  w_cols_leading=False):
    m, d = x_bf.shape
    n_tiles = (m // tm) * col_tiles
    assert n_tiles % 2 == 0 and n == col_tiles * tn

    def mm_tile(t):
        return _lagged_tile(t, n_tiles)[0]

    def done_tile(t):
        return _lagged_tile(t, n_tiles)[1]

    return pl.pallas_call(
        kern,
        grid=(n_tiles + 1,),
        in_specs=[
            pl.BlockSpec((tm, d), lambda t: (mm_tile(t) // col_tiles, 0)),
            (pl.BlockSpec((None, tn, d), lambda t: (layer, mm_tile(t) % col_tiles, 0))
             if w_cols_leading else
             pl.BlockSpec((None, d, tn), lambda t: (layer, 0, mm_tile(t) % col_tiles))),
        ] + [pl.BlockSpec(shape, functools.partial(lambda t, f: f(done_tile(t)), f=f))
             for shape, f in aux_specs],
        out_specs=pl.BlockSpec((tm, tn), lambda t: (done_tile(t) // col_tiles,
                                                    done_tile(t) % col_tiles)),
        out_shape=jax.ShapeDtypeStruct((m, n), BF16),
        scratch_shapes=[
            pltpu.VMEM((2, tm, tn), F32),
            pltpu.VMEM((d, tn) if w_in.dtype != BF16 else (16, LANES), BF16),
        ],
        compiler_params=_cparams(("arbitrary",)),
        name=name,
    )(x_bf, w_in, *aux)


def _fox_in_proj(x_bf, w_in_t, layer, q_gain, k_gain, width, *, tm=2048, tn=512):
    col_tiles = 4 * width // tn
    per_region = width // tn
    gains = jnp.stack([q_gain * (FOX_HEAD_DIM ** -0.5 * LOG2E), k_gain]).reshape(2, 1, -1)
    kern = functools.partial(_fox_in_kernel, col_tiles=col_tiles, mixed_tiles=2 * per_region,
                             plain_tiles=3 * per_region)
    gain_spec = ((None, 1, FOX_HEAD_DIM),
                 lambda done: (jnp.minimum((done % col_tiles) // per_region, 1), 0, 0))
    return _in_proj_call(kern, "fox_in_proj", x_bf, w_in_t, layer, [gains], [gain_spec],
                         4 * width, col_tiles, tm, tn, w_cols_leading=True)


def _ret_in_kernel(x_ref, w_ref, cos_ref, sin_ref, o_ref, acc_ref, wb_ref, **tiles):
    half = RET_QK_DIM // 2

    def rope(acc, rows):
        cos = cos_ref[rows, :]
        sin = sin_ref[rows, :]
        t1 = acc[:, :half]
        t2 = acc[:, half:]
        return [t1 * cos - t2 * sin, t1 * sin + t2 * cos]

    _in_proj_body(x_ref, w_ref, o_ref, acc_ref, wb_ref, rope, **tiles)


def _ret_in_proj(x_bf, w_in, layer, cos2, sin2, seq, qk_width, v_width, *, tm=2048, tn=512):
    assert RET_QK_DIM == MXU_COLS
    n = w_in.shape[2]
    col_tiles = n // tn
    q_tiles = qk_width // tn
    half = RET_QK_DIM // 2
    pos_blocks = seq // tm
    kern = functools.partial(_ret_in_kernel, col_tiles=col_tiles, mixed_tiles=2 * q_tiles,
                             plain_tiles=(2 * qk_width + v_width) // tn)
    table_spec = ((None, tm, half),
                  lambda done: (jnp.minimum((done % col_tiles) // q_tiles, 1),
                                (done // col_tiles) % pos_blocks, 0))
    return _in_proj_call(kern, "ret_in_proj", x_bf, w_in, layer, [cos2, sin2],
                         [table_spec, table_spec], n, col_tiles, tm, tn)


N_SPLIT = 3


def _split3(v):
    hi = v.astype(BF16)
    r = v - hi.astype(F32)
    mid = r.astype(BF16)
    lo = (r - mid.astype(F32)).astype(BF16)
    return hi, mid, lo


def _fgate_kernel(x_ref, w_ref, b_ref, c3_ref, *rest):
    carry_ref = rest[-1]
    ts = x_ref.shape[0]
    n_heads = w_ref.shape[0]

    @pl.when(pl.program_id(1) == 0)
    def _():
        carry_ref[...] = jnp.zeros_like(carry_ref)

    x = x_ref[...].astype(BF16)
    if len(rest) == 2:
        rest[0][...] = x
    z = lax.dot_general(x, w_ref[...].astype(BF16), NT_DIMS,
                        preferred_element_type=F32) + b_ref[...]
    log_f = jnp.minimum(z, 0.0) - jnp.log1p(jnp.exp(-jnp.abs(z)))
    row = lax.broadcasted_iota(jnp.int32, (ts, ts), 0)
    col = lax.broadcasted_iota(jnp.int32, (ts, ts), 1)
    tri = jnp.where(col <= row, 1.0, 0.0).astype(BF16)
    cs = carry_ref[...]
    for part in _split3(log_f):
        cs = cs + jnp.dot(tri, part, preferred_element_type=F32)
    carry_ref[...] = cs[ts - 1:ts, :]
    src = lax.broadcasted_iota(jnp.int32, (n_heads, LANES), 0)
    dst = lax.broadcasted_iota(jnp.int32, (n_heads, LANES), 1)
    c3 = jnp.zeros((ts, LANES), F32)
    for t, part in enumerate(_split3(cs * LOG2E)):
        place = jnp.where(dst == t * n_heads + src, 1.0, 0.0).astype(BF16)
        c3 = c3 + jnp.dot(part, place, preferred_element_type=F32)
    c3_ref[...] = c3.astype(c3_ref.dtype)


def _fox_forget_cumsum(x, w_in_t, layer, b_f, batch, seq, f_col0, *, ts=512):
    m, d = x.shape
    h = b_f.shape[0]
    ns = seq // ts
    assert f_col0 % h == 0 and w_in_t.shape[1] == f_col0 + h and N_SPLIT * h <= LANES
    row_block = lambda bi, si: (bi * ns + si, 0)
    out_specs = [pl.BlockSpec((ts, LANES), row_block)]
    out_shape = [jax.ShapeDtypeStruct((m, LANES), BF16)]
    if x.dtype != BF16:
        out_specs.append(pl.BlockSpec((ts, d), row_block))
        out_shape.append(jax.ShapeDtypeStruct((m, d), BF16))
    outs = pl.pallas_call(
        _fgate_kernel,
        grid=(batch, ns),
        in_specs=[
            pl.BlockSpec((ts, d), row_block),
            pl.BlockSpec((None, h, d), lambda bi, si: (layer, f_col0 // h, 0)),
            pl.BlockSpec((1, h), lambda bi, si: (0, 0)),
        ],
        out_specs=out_specs,
        out_shape=out_shape,
        scratch_shapes=[pltpu.VMEM((1, h), F32)],
        compiler_params=_cparams(("parallel", "arbitrary")),
        name="fox_forget_cumsum",
    )(x, w_in_t, b_f.reshape(1, -1))
    return outs if len(outs) == 2 else outs[0]


def _fox_attn_kernel(q_ref, k_ref, v_ref, g_ref, c3_ref, wo_ref, o_ref, wob_ref,
                     qa_ref, kaug_ref, vaug_ref, m_ref, acc_ref, p_ref, alpha_ref,
                     *, blk, n_heads):
    h = pl.program_id(1)
    wob_ref[...] = wo_ref[...].astype(wob_ref.dtype)
    s_len = k_ref.shape[0]
    dh = FOX_HEAD_DIM
    n_blk = s_len // blk

    src = lax.broadcasted_iota(jnp.int32, (LANES, 2 * dh), 0)
    dst = lax.broadcasted_iota(jnp.int32, (LANES, 2 * dh), 1)
    to_q = (dst < N_SPLIT) & (src == dst * n_heads + h)
    dk = dst - (dh + N_SPLIT)
    to_k = (dk >= 0) & (dk < N_SPLIT) & (src == dk * n_heads + h)
    place = (jnp.where(to_q, 1.0, 0.0) - jnp.where(to_k, 1.0, 0.0)).astype(BF16)
    x = jnp.dot(c3_ref[...], place, preferred_element_type=F32)
    lane = lax.broadcasted_iota(jnp.int32, (s_len, dh), 1)
    qa_ref[:, :dh] = q_ref[...]
    qa_ref[:, dh:] = (x[:, :dh] + jnp.where((lane >= N_SPLIT) & (lane < 2 * N_SPLIT), 1.0, 0.0)
                      ).astype(BF16)
    kaug_ref[:, :dh] = k_ref[...]
    kaug_ref[:, dh:] = (x[:, dh:] + jnp.where(lane < N_SPLIT, 1.0, 0.0)).astype(BF16)
    vaug_ref[:, :dh] = v_ref[...]
    vaug_ref[:, dh:] = jnp.ones((s_len, dh), BF16)
    m_ref[...] = jnp.full(m_ref.shape, -jnp.inf, F32)
    acc_ref[...] = jnp.zeros_like(acc_ref)

    def qk_softmax(slot, r0, nr, k0, nk, last):
        rows = pl.ds(r0, nr)
        s = lax.dot_general(qa_ref[rows, :], kaug_ref[pl.ds(k0, nk), :], NT_DIMS,
                            preferred_element_type=F32)
        if last:
            row = lax.broadcasted_iota(jnp.int32, (nr, nk), 0) + r0
            col = lax.broadcasted_iota(jnp.int32, (nr, nk), 1) + k0
            s = jnp.where(col <= row, s, -jnp.inf)
        m_old = m_ref[rows, :]
        m_new = jnp.maximum(m_old, jnp.max(s, axis=-1, keepdims=True))
        alpha_ref[slot, :nr, :] = jnp.exp2(m_old - m_new)
        p_ref[slot, :nr, :nk] = jnp.exp2(
            s - jnp.concatenate([m_new] * (nk // LANES), axis=1)).astype(BF16)
        m_ref[rows, :] = m_new

    def pv_update(slot, r0, nr, k0, nk, last):
        rows = pl.ds(r0, nr)
        pv = jnp.dot(p_ref[slot, :nr, :nk], vaug_ref[pl.ds(k0, nk), :],
                     preferred_element_type=F32)
        alpha = alpha_ref[slot, :nr, :]
        acc_ref[rows, :] = jnp.concatenate([alpha, alpha], axis=1) * acc_ref[rows, :] + pv
        if last:
            o_ref[rows, :] = (acc_ref[rows, :dh] / acc_ref[rows, dh:]
                              * g_ref[rows, :].astype(F32)).astype(o_ref.dtype)

    half = blk // 2
    tasks = []
    for qi in range(n_blk):
        q0 = qi * blk
        tasks += [(q0, blk, ki * blk, blk, False) for ki in range(qi)]
        tasks += [(q0, half, q0, half, True), (q0 + half, half, q0, blk, True)]
    for t, task in enumerate(tasks):
        qk_softmax(t % 2, *task)
        if t > 0:
            pv_update((t - 1) % 2, *tasks[t - 1])
    pv_update((len(tasks) - 1) % 2, *tasks[-1])


def _fox_attention(proj, c3, w_out, layer, batch, seq, *, blk=512):
    m, n = proj.shape
    k_out, d_out = w_out.shape[1:]
    w_rows = k_out // (batch * (n // 4 // FOX_HEAD_DIM))
    width = n // 4
    heads = width // FOX_HEAD_DIM
    dh = FOX_HEAD_DIM
    kern = functools.partial(_fox_attn_kernel, blk=blk, n_heads=heads)
    return pl.pallas_call(
        kern,
        grid=(batch, heads),
        in_specs=[
            pl.BlockSpec((seq, dh), lambda b, h: (b, h)),
            pl.BlockSpec((seq, dh), lambda b, h: (b, heads + h)),
            pl.BlockSpec((seq, dh), lambda b, h: (b, 2 * heads + h)),
            pl.BlockSpec((seq, dh), lambda b, h: (b, 3 * heads + h)),
            pl.BlockSpec((seq, LANES), lambda b, h: (b, 0)),
            pl.BlockSpec((None, w_rows, d_out), lambda b, h: (layer, b * heads + h, 0)),
        ],
        out_specs=[pl.BlockSpec((seq, dh), lambda b, h: (b, h)),
                   pl.BlockSpec((w_rows, d_out), lambda b, h: (b * heads + h, 0))],
        out_shape=[jax.ShapeDtypeStruct((m, width), BF16),
                   jax.ShapeDtypeStruct((k_out, d_out), BF16)],
        scratch_shapes=[
            pltpu.VMEM((seq, 2 * dh), BF16),
            pltpu.VMEM((seq, 2 * dh), BF16),
            pltpu.VMEM((seq, 2 * dh), BF16),
            pltpu.VMEM((seq, LANES), F32),
            pltpu.VMEM((seq, 2 * dh), F32),
            pltpu.VMEM((2, blk, blk), BF16),
            pltpu.VMEM((2, blk, LANES), F32),
        ],
        compiler_params=_cparams(("parallel", "parallel")),
        name="fox_attention",
    )(proj, proj, proj, proj, c3, w_out)


def _out_ln_kernel(y_ref, w_ref, h_ref, g_ref, b_ref, o_ref, obf_ref, *, alpha, row_group):
    for r in range(0, y_ref.shape[0], row_group):
        rows = slice(r, r + row_group)
        br = jnp.dot(y_ref[rows, :], w_ref[...], preferred_element_type=F32)
        z = alpha * h_ref[rows, :] + br
        mu = jnp.mean(z, axis=-1, keepdims=True)
        zc = z - mu
        var = jnp.mean(zc * zc, axis=-1, keepdims=True)
        out = zc * lax.rsqrt(var + LN_EPS) * g_ref[...] + b_ref[...]
        o_ref[rows, :] = out
        obf_ref[rows, :] = out.astype(obf_ref.dtype)


def _out_proj_ln(y_bf, w_bf, h, gain, bias, alpha, *, tm=512, row_group=256):
    m, k = y_bf.shape
    d = w_bf.shape[1]
    kern = functools.partial(_out_ln_kernel, alpha=alpha, row_group=row_group)
    return pl.pallas_call(
        kern,
        grid=(m // tm,),
        in_specs=[
            pl.BlockSpec((tm, k), lambda i: (i, 0)),
            pl.BlockSpec((k, d), lambda i: (0, 0), pipeline_mode=pl.Buffered(1)),
            pl.BlockSpec((tm, d), lambda i: (i, 0)),
            pl.BlockSpec((1, d), lambda i: (0, 0)),
            pl.BlockSpec((1, d), lambda i: (0, 0)),
        ],
        out_specs=[pl.BlockSpec((tm, d), lambda i: (i, 0)),
                   pl.BlockSpec((tm, d), lambda i: (i, 0))],
        out_shape=[jax.ShapeDtypeStruct((m, d), F32), jax.ShapeDtypeStruct((m, d), BF16)],
        compiler_params=_cparams(("parallel",)),
        name="out_proj_ln",
    )(y_bf, w_bf, h, gain.reshape(1, -1), bias.reshape(1, -1))


def _retention_kernel(lg_ref, q_ref, k_ref, v_ref, g_ref, wo_ref, o_ref, wob_ref, r_ref, *, chunk):
    h = pl.program_id(1)
    n = pl.program_id(2)
    wob_ref[...] = wo_ref[...].astype(wob_ref.dtype)

    @pl.when(n == 0)
    def _():
        r_ref[...] = jnp.zeros_like(r_ref)

    lg = lg_ref[h]
    row = lax.broadcasted_iota(jnp.int32, (chunk, chunk), 0)
    col = lax.broadcasted_iota(jnp.int32, (chunk, chunk), 1)
    diff = (row - col).astype(F32)
    intra = jnp.where(diff >= 0, jnp.exp(jnp.maximum(diff, 0.0) * lg), 0.0)
    pos = lax.broadcasted_iota(jnp.int32, (chunk, 1), 0).astype(F32)
    q_decay = jnp.exp((pos + 1.0) * lg)
    k_decay = jnp.exp((chunk - 1.0 - pos) * lg)
    c_decay = jnp.exp(jnp.full((1, 1), chunk, F32) * lg)

    r = r_ref[...]
    for ci in range(q_ref.shape[0] // chunk):
        rows = slice(ci * chunk, (ci + 1) * chunk)
        q = q_ref[rows, :]
        k = k_ref[rows, :]
        v = v_ref[rows, :]
        inner = lax.dot_general(q, k, NT_DIMS, preferred_element_type=F32) * intra
        o = (jnp.dot(inner.astype(BF16), v, preferred_element_type=F32)
             + jnp.dot(q, r.astype(BF16), preferred_element_type=F32) * q_decay)
        kd = (k.astype(F32) * k_decay).astype(BF16)
        r = r * c_decay + lax.dot_general(kd, v, TN_DIMS, preferred_element_type=F32)

        mu = jnp.mean(o, axis=-1, keepdims=True)
        oc = o - mu
        var = jnp.mean(oc * oc, axis=-1, keepdims=True)
        o_ref[rows, :] = (oc * lax.rsqrt(var + GN_EPS)
                          * g_ref[rows, :].astype(F32)).astype(o_ref.dtype)
    r_ref[...] = r


def _retention(proj, log_gamma, w_out, layer, batch, seq, heads, *, chunk=256,
               chunks_per_step=16):
    m = proj.shape[0]
    ts = chunk * chunks_per_step
    ns = seq // ts
    dk, dv = RET_QK_DIM, RET_V_DIM
    v_off = 2 * heads * dk // dv
    g_off = v_off + heads
    k_out, d_out = w_out.shape[1:]
    w_rows = k_out // (batch * heads * ns)
    step = lambda b, h, n: (b * heads + h) * ns + n
    kern = functools.partial(_retention_kernel, chunk=chunk)
    return pl.pallas_call(
        kern,
        grid_spec=pltpu.PrefetchScalarGridSpec(
            num_scalar_prefetch=1,
            grid=(batch, heads, ns),
            in_specs=[
                pl.BlockSpec((ts, dk), lambda b, h, n, lg: (b * ns + n, h)),
                pl.BlockSpec((ts, dk), lambda b, h, n, lg: (b * ns + n, heads + h)),
                pl.BlockSpec((ts, dv), lambda b, h, n, lg: (b * ns + n, v_off + h)),
                pl.BlockSpec((ts, dv), lambda b, h, n, lg: (b * ns + n, g_off + h)),
                pl.BlockSpec((None, w_rows, d_out), lambda b, h, n, lg: (layer, step(b, h, n), 0)),
            ],
            out_specs=[pl.BlockSpec((ts, dv), lambda b, h, n, lg: (b * ns + n, h)),
                       pl.BlockSpec((w_rows, d_out), lambda b, h, n, lg: (step(b, h, n), 0))],
            scratch_shapes=[pltpu.VMEM((dk, dv), F32)],
        ),
        out_shape=[jax.ShapeDtypeStruct((m, heads * dv), BF16),
                   jax.ShapeDtypeStruct((k_out, d_out), BF16)],
        compiler_params=_cparams(("parallel", "parallel", "arbitrary")),
        name="retention",
    )(log_gamma, proj, proj, proj, proj, w_out)


def kernel(x, fox_w_in, fox_b_f, fox_q_gain, fox_k_gain, fox_w_out, ret_w_in, ret_w_out,
           ln_gain, ln_bias):
    batch, seq, d_model = x.shape
    depth = ln_gain.shape[0]
    m = batch * seq
    alpha = (2.0 * depth) ** 0.25

    fox_width = fox_w_out.shape[1]
    ret_v_width = ret_w_out.shape[1]
    ret_heads = ret_v_width // RET_V_DIM
    ret_qk_width = ret_heads * RET_QK_DIM

    inv_freq = ROPE_BASE ** (-jnp.arange(0, RET_QK_DIM, 2, dtype=F32) / RET_QK_DIM)
    ang = jnp.arange(seq, dtype=F32)[:, None] * inv_freq[None, :]
    k_scale = RET_QK_DIM ** -0.5
    cos2 = jnp.stack([jnp.cos(ang), jnp.cos(ang) * k_scale])
    sin2 = jnp.stack([jnp.sin(ang), jnp.sin(ang) * k_scale])
    log_gamma = jnp.log1p(-jnp.exp2(-5.0 - jnp.arange(ret_heads, dtype=F32)))

    fox_w_in_t = jnp.swapaxes(fox_w_in, 1, 2)
    h = x.reshape(m, d_model)
    h_bf = None
    for i in range(depth):
        j = i // 2
        if i % 2 == 0:
            if h_bf is None:
                c3, h_bf = _fox_forget_cumsum(h, fox_w_in_t, j, fox_b_f[j], batch, seq,
                                              4 * fox_width)
            else:
                c3 = _fox_forget_cumsum(h_bf, fox_w_in_t, j, fox_b_f[j], batch, seq,
                                        4 * fox_width)
            proj = _fox_in_proj(h_bf, fox_w_in_t, j, fox_q_gain[j], fox_k_gain[j], fox_width)
            y, w_out = _fox_attention(proj, c3, fox_w_out, j, batch, seq)
        else:
            proj = _ret_in_proj(h_bf, ret_w_in, j, cos2, sin2, seq, ret_qk_width, ret_v_width)
            y, w_out = _retention(proj, log_gamma, ret_w_out, j, batch, seq, ret_heads)
        h, h_bf = _out_proj_ln(y, w_out, h, ln_gain[i], ln_bias[i], alpha)
    return h.reshape(batch, seq, d_model)
```

```python
import functools
import math

import jax
import jax.numpy as jnp
from jax import lax
from jax.experimental import pallas as pl
from jax.experimental.pallas import tpu as pltpu

F32 = jnp.float32
BF16 = jnp.bfloat16

LANES = 128
MXU_COLS = 256
IN_PROJ_ROW_CHUNK = 512
EPILOGUE_ROWS = 32

FOX_HEAD_DIM = 128
RET_QK_DIM = 256
RET_V_DIM = 512
ROPE_BASE = 10000.0
LN_EPS = 1e-5
GN_EPS = 1e-6
QK_EPS = 1e-6
LOG2E = math.log2(math.e)

VMEM_LIMIT = 56 * 1024 * 1024

NT_DIMS = (((1,), (1,)), ((), ()))
TN_DIMS = (((0,), (0,)), ((), ()))


def _cparams(sem):
    return pltpu.CompilerParams(dimension_semantics=sem, vmem_limit_bytes=VMEM_LIMIT)


def _silu(g):
    return g * (1.0 / (1.0 + jnp.exp(-g)))


def _lagged_tile(t, n_tiles):
    return jnp.minimum(t, n_tiles - 1), jnp.maximum(t, 1) - 1


def _in_proj_body(x_ref, w_ref, o_ref, acc_ref, wb_ref, mixed_epilogue, *,
                  col_tiles, mixed_tiles, plain_tiles):
    t = pl.program_id(0)
    n_tiles = pl.num_programs(0) - 1
    _, done = _lagged_tile(t, n_tiles)
    jd = done % col_tiles
    tm, tn = o_ref.shape

    def step(mm_slot, done_slot, first_and_last):
        def run(do_matmul, epilogue):
            w_src = w_ref
            if do_matmul and w_ref.dtype != BF16:
                w = w_ref[...]
                if w.shape != wb_ref.shape:
                    w = w.T
                wb_ref[...] = w.astype(BF16)
                w_src = wb_ref
            for r in range(0, tm, IN_PROJ_ROW_CHUNK):
                rows = slice(r, r + IN_PROJ_ROW_CHUNK)
                if do_matmul:
                    acc_ref[mm_slot, rows, :] = jnp.dot(x_ref[rows, :], w_src[...],
                                                        preferred_element_type=F32)
                if epilogue is not None:
                    for rr in range(r, r + IN_PROJ_ROW_CHUNK, EPILOGUE_ROWS):
                        sub = slice(rr, rr + EPILOGUE_ROWS)
                        for c in range(tn // MXU_COLS):
                            cols = slice(c * MXU_COLS, (c + 1) * MXU_COLS)
                            col = c * MXU_COLS
                            for piece in epilogue(acc_ref[done_slot, sub, cols], sub):
                                o_ref[sub, col:col + piece.shape[1]] = piece.astype(o_ref.dtype)
                                col += piece.shape[1]

        running = (t > 0) & (t < n_tiles)
        kinds = [(jd < mixed_tiles, mixed_epilogue),
                 ((jd >= mixed_tiles) & (jd < plain_tiles), lambda acc, rows: [acc]),
                 (jd >= plain_tiles, lambda acc, rows: [_silu(acc)])]
        for cond, epilogue in kinds:
            pl.when(running & cond)(functools.partial(run, True, epilogue))
        if first_and_last:
            pl.when(t == 0)(functools.partial(run, True, None))
            pl.when(t == n_tiles)(functools.partial(run, False, kinds[-1][1]))

    @pl.when(t % 2 == 0)
    def _():
        step(0, 1, True)

    @pl.when(t % 2 == 1)
    def _():
        step(1, 0, False)


def _fox_in_kernel(x_ref, w_ref, g_ref, o_ref, acc_ref, wb_ref, **tiles):
    def rms(acc, rows):
        outs = []
        for hh in range(MXU_COLS // FOX_HEAD_DIM):
            a = acc[:, hh * FOX_HEAD_DIM:(hh + 1) * FOX_HEAD_DIM]
            ms = jnp.mean(a * a, axis=-1, keepdims=True)
            outs.append(a * lax.rsqrt(ms + QK_EPS) * g_ref[...])
        return outs

    _in_proj_body(x_ref, w_ref, o_ref, acc_ref, wb_ref, rms, **tiles)


def _in_proj_call(kern, name, x_bf, w_in, layer, aux, aux_specs, n, col_tiles, tm, tn,
                  w_cols_leading=False):
    m, d = x_bf.shape
    n_tiles = (m // tm) * col_tiles
    assert n_tiles % 2 == 0 and n == col_tiles * tn

    def mm_tile(t):
        return _lagged_tile(t, n_tiles)[0]

    def done_tile(t):
        return _lagged_tile(t, n_tiles)[1]

    return pl.pallas_call(
        kern,
        grid=(n_tiles + 1,),
        in_specs=[
            pl.BlockSpec((tm, d), lambda t: (mm_tile(t) // col_tiles, 0)),
            (pl.BlockSpec((None, tn, d), lambda t: (layer, mm_tile(t) % col_tiles, 0))
             if w_cols_leading else
             pl.BlockSpec((None, d, tn), lambda t: (layer, 0, mm_tile(t) % col_tiles))),
        ] + [pl.BlockSpec(shape, functools.partial(lambda t, f: f(done_tile(t)), f=f))
             for shape, f in aux_specs],
        out_specs=pl.BlockSpec((tm, tn), lambda t: (done_tile(t) // col_tiles,
                                                    done_tile(t) % col_tiles)),
        out_shape=jax.ShapeDtypeStruct((m, n), BF16),
        scratch_shapes=[
            pltpu.VMEM((2, tm, tn), F32),
            pltpu.VMEM((d, tn) if w_in.dtype != BF16 else (16, LANES), BF16),
        ],
        compiler_params=_cparams(("arbitrary",)),
        name=name,
    )(x_bf, w_in, *aux)


def _fox_in_proj(x_bf, w_in_t, layer, q_gain, k_gain, width, *, tm=2048, tn=512):
    col_tiles = 4 * width // tn
    per_region = width // tn
    gains = jnp.stack([q_gain * (FOX_HEAD_DIM ** -0.5 * LOG2E), k_gain]).reshape(2, 1, -1)
    kern = functools.partial(_fox_in_kernel, col_tiles=col_tiles, mixed_tiles=2 * per_region,
                             plain_tiles=3 * per_region)
    gain_spec = ((None, 1, FOX_HEAD_DIM),
                 lambda done: (jnp.minimum((done % col_tiles) // per_region, 1), 0, 0))
    return _in_proj_call(kern, "fox_in_proj", x_bf, w_in_t, layer, [gains], [gain_spec],
                         4 * width, col_tiles, tm, tn, w_cols_leading=True)


def _ret_in_kernel(x_ref, w_ref, cos_ref, sin_ref, o_ref, acc_ref, wb_ref, **tiles):
    half = RET_QK_DIM // 2

    def rope(acc, rows):
        cos = cos_ref[rows, :]
        sin = sin_ref[rows, :]
        t1 = acc[:, :half]
        t2 = acc[:, half:]
        return [t1 * cos - t2 * sin, t1 * sin + t2 * cos]

    _in_proj_body(x_ref, w_ref, o_ref, acc_ref, wb_ref, rope, **tiles)


def _ret_in_proj(x_bf, w_in, layer, cos2, sin2, seq, qk_width, v_width, *, tm=2048, tn=512):
    assert RET_QK_DIM == MXU_COLS
    n = w_in.shape[2]
    col_tiles = n // tn
    q_tiles = qk_width // tn
    half = RET_QK_DIM // 2
    pos_blocks = seq // tm
    kern = functools.partial(_ret_in_kernel, col_tiles=col_tiles, mixed_tiles=2 * q_tiles,
                             plain_tiles=(2 * qk_width + v_width) // tn)
    table_spec = ((None, tm, half),
                  lambda done: (jnp.minimum((done % col_tiles) // q_tiles, 1),
                                (done // col_tiles) % pos_blocks, 0))
    return _in_proj_call(kern, "ret_in_proj", x_bf, w_in, layer, [cos2, sin2],
                         [table_spec, table_spec], n, col_tiles, tm, tn)


N_SPLIT = 3


def _split3(v):
    hi = v.astype(BF16)
    r = v - hi.astype(F32)
    mid = r.astype(BF16)
    lo = (r - mid.astype(F32)).astype(BF16)
    return hi, mid, lo


def _fgate_kernel(x_ref, w_ref, b_ref, c3_ref, *rest):
    carry_ref = rest[-1]
    ts = x_ref.shape[0]
    n_heads = w_ref.shape[0]

    @pl.when(pl.program_id(1) == 0)
    def _():
        carry_ref[...] = jnp.zeros_like(carry_ref)

    x = x_ref[...].astype(BF16)
    if len(rest) == 2:
        rest[0][...] = x
    z = lax.dot_general(w_ref[...].astype(BF16), x, NT_DIMS,
                        preferred_element_type=F32) + b_ref[...]
    log_f = jnp.minimum(z, 0.0) - jnp.log1p(jnp.exp(-jnp.abs(z)))
    row = lax.broadcasted_iota(jnp.int32, (ts, ts), 0)
    col = lax.broadcasted_iota(jnp.int32, (ts, ts), 1)
    tri = jnp.where(row <= col, 1.0, 0.0).astype(BF16)
    cs = carry_ref[...]
    for part in _split3(log_f):
        cs = cs + jnp.dot(part, tri, preferred_element_type=F32)
    carry_ref[...] = cs[:, ts - 1:ts]
    parts = [p.astype(F32) for p in _split3(cs * LOG2E)]
    pad = jnp.zeros((LANES - N_SPLIT * n_heads, ts), F32)
    c3_ref[...] = jnp.concatenate(parts + [pad], axis=0).T.astype(c3_ref.dtype)


def _fox_forget_cumsum(x, w_in_t, layer, b_f, batch, seq, f_col0, *, ts=512):
    m, d = x.shape
    h = b_f.shape[0]
    ns = seq // ts
    assert f_col0 % h == 0 and w_in_t.shape[1] == f_col0 + h and N_SPLIT * h <= LANES
    row_block = lambda bi, si: (bi * ns + si, 0)
    out_specs = [pl.BlockSpec((ts, LANES), row_block)]
    out_shape = [jax.ShapeDtypeStruct((m, LANES), BF16)]
    if x.dtype != BF16:
        out_specs.append(pl.BlockSpec((ts, d), row_block))
        out_shape.append(jax.ShapeDtypeStruct((m, d), BF16))
    outs = pl.pallas_call(
        _fgate_kernel,
        grid=(batch, ns),
        in_specs=[
            pl.BlockSpec((ts, d), row_block),
            pl.BlockSpec((None, h, d), lambda bi, si: (layer, f_col0 // h, 0)),
            pl.BlockSpec((h, 1), lambda bi, si: (0, 0)),
        ],
        out_specs=out_specs,
        out_shape=out_shape,
        scratch_shapes=[pltpu.VMEM((h, 1), F32)],
        compiler_params=_cparams(("parallel", "arbitrary")),
        name="fox_forget_cumsum",
    )(x, w_in_t, b_f.reshape(-1, 1))
    return outs if len(outs) == 2 else outs[0]


def _fox_attn_kernel(q_ref, k_ref, v_ref, g_ref, c3_ref, wo_ref, o_ref, wob_ref,
                     qa_ref, kaug_ref, vaug_ref, m_ref, acc_ref, p_ref, alpha_ref,
                     *, blk, n_heads):
    h = pl.program_id(1)
    wob_ref[...] = wo_ref[...].astype(wob_ref.dtype)
    s_len = k_ref.shape[0]
    dh = FOX_HEAD_DIM
    n_blk = s_len // blk

    src = lax.broadcasted_iota(jnp.int32, (LANES, 2 * dh), 0)
    dst = lax.broadcasted_iota(jnp.int32, (LANES, 2 * dh), 1)
    to_q = (dst < N_SPLIT) & (src == dst * n_heads + h)
    dk = dst - (dh + N_SPLIT)
    to_k = (dk >= 0) & (dk < N_SPLIT) & (src == dk * n_heads + h)
    place = (jnp.where(to_q, 1.0, 0.0) - jnp.where(to_k, 1.0, 0.0)).astype(BF16)
    x = jnp.dot(c3_ref[...], place, preferred_element_type=F32)
    lane = lax.broadcasted_iota(jnp.int32, (s_len, dh), 1)
    qa_ref[:, :dh] = q_ref[...]
    qa_ref[:, dh:] = (x[:, :dh] + jnp.where((lane >= N_SPLIT) & (lane < 2 * N_SPLIT), 1.0, 0.0)
                      ).astype(BF16)
    kaug_ref[:, :dh] = k_ref[...]
    kaug_ref[:, dh:] = (x[:, dh:] + jnp.where(lane < N_SPLIT, 1.0, 0.0)).astype(BF16)
    vaug_ref[:, :dh] = v_ref[...]
    vaug_ref[:, dh:] = jnp.ones((s_len, dh), BF16)
    m_ref[...] = jnp.full(m_ref.shape, -jnp.inf, F32)
    acc_ref[...] = jnp.zeros_like(acc_ref)

    def qk_softmax(slot, r0, nr, k0, nk, last):
        rows = pl.ds(r0, nr)
        s = lax.dot_general(qa_ref[rows, :], kaug_ref[pl.ds(k0, nk), :], NT_DIMS,
                            preferred_element_type=F32)
        if last:
            row = lax.broadcasted_iota(jnp.int32, (nr, nk), 0) + r0
            col = lax.broadcasted_iota(jnp.int32, (nr, nk), 1) + k0
            s = jnp.where(col <= row, s, -jnp.inf)
        m_old = m_ref[rows, :]
        m_new = jnp.maximum(m_old, jnp.max(s, axis=-1, keepdims=True))
        alpha_ref[slot, :nr, :] = jnp.exp2(m_old - m_new)
        p_ref[slot, :nr, :nk] = jnp.exp2(
            s - jnp.concatenate([m_new] * (nk // LANES), axis=1)).astype(BF16)
        m_ref[rows, :] = m_new

    def pv_update(slot, r0, nr, k0, nk, last):
        rows = pl.ds(r0, nr)
        pv = jnp.dot(p_ref[slot, :nr, :nk], vaug_ref[pl.ds(k0, nk), :],
                     preferred_element_type=F32)
        alpha = alpha_ref[slot, :nr, :]
        acc_ref[rows, :] = jnp.concatenate([alpha, alpha], axis=1) * acc_ref[rows, :] + pv
        if last:
            o_ref[rows, :] = (acc_ref[rows, :dh] / acc_ref[rows, dh:]
                              * g_ref[rows, :].astype(F32)).astype(o_ref.dtype)

    half = blk // 2
    tasks = []
    for qi in range(n_blk):
        q0 = qi * blk
        tasks += [(q0, blk, ki * blk, blk, False) for ki in range(qi)]
        tasks += [(q0, half, q0, half, True), (q0 + half, half, q0, blk, True)]
    for t, task in enumerate(tasks):
        qk_softmax(t % 2, *task)
        if t > 0:
            pv_update((t - 1) % 2, *tasks[t - 1])
    pv_update((len(tasks) - 1) % 2, *tasks[-1])


def _fox_attention(proj, c3, w_out, layer, batch, seq, *, blk=512):
    m, n = proj.shape
    k_out, d_out = w_out.shape[1:]
    w_rows = k_out // (batch * (n // 4 // FOX_HEAD_DIM))
    width = n // 4
    heads = width // FOX_HEAD_DIM
    dh = FOX_HEAD_DIM
    kern = functools.partial(_fox_attn_kernel, blk=blk, n_heads=heads)
    return pl.pallas_call(
        kern,
        grid=(batch, heads),
        in_specs=[
            pl.BlockSpec((seq, dh), lambda b, h: (b, h)),
            pl.BlockSpec((seq, dh), lambda b, h: (b, heads + h)),
            pl.BlockSpec((seq, dh), lambda b, h: (b, 2 * heads + h)),
            pl.BlockSpec((seq, dh), lambda b, h: (b, 3 * heads + h)),
            pl.BlockSpec((seq, LANES), lambda b, h: (b, 0)),
            pl.BlockSpec((None, w_rows, d_out), lambda b, h: (layer, b * heads + h, 0)),
        ],
        out_specs=[pl.BlockSpec((seq, dh), lambda b, h: (b, h)),
                   pl.BlockSpec((w_rows, d_out), lambda b, h: (b * heads + h, 0))],
        out_shape=[jax.ShapeDtypeStruct((m, width), BF16),
                   jax.ShapeDtypeStruct((k_out, d_out), BF16)],
        scratch_shapes=[
            pltpu.VMEM((seq, 2 * dh), BF16),
            pltpu.VMEM((seq, 2 * dh), BF16),
            pltpu.VMEM((seq, 2 * dh), BF16),
            pltpu.VMEM((seq, LANES), F32),
            pltpu.VMEM((seq, 2 * dh), F32),
            pltpu.VMEM((2, blk, blk), BF16),
            pltpu.VMEM((2, blk, LANES), F32),
        ],
        compiler_params=_cparams(("parallel", "parallel")),
        name="fox_attention",
    )(proj, proj, proj, proj, c3, w_out)


def _out_ln_kernel(y_ref, w_ref, h_ref, g_ref, b_ref, o_ref, obf_ref, *, alpha, row_group):
    for r in range(0, y_ref.shape[0], row_group):
        rows = slice(r, r + row_group)
        br = jnp.dot(y_ref[rows, :], w_ref[...], preferred_element_type=F32)
        z = alpha * h_ref[rows, :] + br
        mu = jnp.mean(z, axis=-1, keepdims=True)
        zc = z - mu
        var = jnp.mean(zc * zc, axis=-1, keepdims=True)
        out = zc * lax.rsqrt(var + LN_EPS) * g_ref[...] + b_ref[...]
        o_ref[rows, :] = out
        obf_ref[rows, :] = out.astype(obf_ref.dtype)


def _out_proj_ln(y_bf, w_bf, h, gain, bias, alpha, *, tm=512, row_group=256):
    m, k = y_bf.shape
    d = w_bf.shape[1]
    kern = functools.partial(_out_ln_kernel, alpha=alpha, row_group=row_group)
    return pl.pallas_call(
        kern,
        grid=(m // tm,),
        in_specs=[
            pl.BlockSpec((tm, k), lambda i: (i, 0)),
            pl.BlockSpec((k, d), lambda i: (0, 0), pipeline_mode=pl.Buffered(1)),
            pl.BlockSpec((tm, d), lambda i: (i, 0)),
            pl.BlockSpec((1, d), lambda i: (0, 0)),
            pl.BlockSpec((1, d), lambda i: (0, 0)),
        ],
        out_specs=[pl.BlockSpec((tm, d), lambda i: (i, 0)),
                   pl.BlockSpec((tm, d), lambda i: (i, 0))],
        out_shape=[jax.ShapeDtypeStruct((m, d), F32), jax.ShapeDtypeStruct((m, d), BF16)],
        compiler_params=_cparams(("parallel",)),
        name="out_proj_ln",
    )(y_bf, w_bf, h, gain.reshape(1, -1), bias.reshape(1, -1))


def _retention_kernel(lg_ref, q_ref, k_ref, v_ref, g_ref, wo_ref, o_ref, wob_ref, r_ref, *, chunk):
    h = pl.program_id(1)
    n = pl.program_id(2)
    wob_ref[...] = wo_ref[...].astype(wob_ref.dtype)

    @pl.when(n == 0)
    def _():
        r_ref[...] = jnp.zeros_like(r_ref)

    lg = lg_ref[h]
    row = lax.broadcasted_iota(jnp.int32, (chunk, chunk), 0)
    col = lax.broadcasted_iota(jnp.int32, (chunk, chunk), 1)
    diff = (row - col).astype(F32)
    intra = jnp.where(diff >= 0, jnp.exp(jnp.maximum(diff, 0.0) * lg), 0.0)
    pos = lax.broadcasted_iota(jnp.int32, (chunk, 1), 0).astype(F32)
    q_decay = jnp.exp((pos + 1.0) * lg)
    k_decay = jnp.exp((chunk - 1.0 - pos) * lg)
    c_decay = jnp.exp(jnp.full((1, 1), chunk, F32) * lg)

    r = r_ref[...]
    for ci in range(q_ref.shape[0] // chunk):
        rows = slice(ci * chunk, (ci + 1) * chunk)
        q = q_ref[rows, :]
        k = k_ref[rows, :]
        v = v_ref[rows, :]
        inner = lax.dot_general(q, k, NT_DIMS, preferred_element_type=F32) * intra
        o = (jnp.dot(inner.astype(BF16), v, preferred_element_type=F32)
             + jnp.dot(q, r.astype(BF16), preferred_element_type=F32) * q_decay)
        kd = (k.astype(F32) * k_decay).astype(BF16)
        r = r * c_decay + lax.dot_general(kd, v, TN_DIMS, preferred_element_type=F32)

        mu = jnp.mean(o, axis=-1, keepdims=True)
        oc = o - mu
        var = jnp.mean(oc * oc, axis=-1, keepdims=True)
        o_ref[rows, :] = (oc * lax.rsqrt(var + GN_EPS)
                          * g_ref[rows, :].astype(F32)).astype(o_ref.dtype)
    r_ref[...] = r


def _retention(proj, log_gamma, w_out, layer, batch, seq, heads, *, chunk=256,
               chunks_per_step=16):
    m = proj.shape[0]
    ts = chunk * chunks_per_step
    ns = seq // ts
    dk, dv = RET_QK_DIM, RET_V_DIM
    v_off = 2 * heads * dk // dv
    g_off = v_off + heads
    k_out, d_out = w_out.shape[1:]
    w_rows = k_out // (batch * heads * ns)
    step = lambda b, h, n: (b * heads + h) * ns + n
    kern = functools.partial(_retention_kernel, chunk=chunk)
    return pl.pallas_call(
        kern,
        grid_spec=pltpu.PrefetchScalarGridSpec(
            num_scalar_prefetch=1,
            grid=(batch, heads, ns),
            in_specs=[
                pl.BlockSpec((ts, dk), lambda b, h, n, lg: (b * ns + n, h)),
                pl.BlockSpec((ts, dk), lambda b, h, n, lg: (b * ns + n, heads + h)),
                pl.BlockSpec((ts, dv), lambda b, h, n, lg: (b * ns + n, v_off + h)),
                pl.BlockSpec((ts, dv), lambda b, h, n, lg: (b * ns + n, g_off + h)),
                pl.BlockSpec((None, w_rows, d_out), lambda b, h, n, lg: (layer, step(b, h, n), 0)),
            ],
            out_specs=[pl.BlockSpec((ts, dv), lambda b, h, n, lg: (b * ns + n, h)),
                       pl.BlockSpec((w_rows, d_out), lambda b, h, n, lg: (step(b, h, n), 0))],
            scratch_shapes=[pltpu.VMEM((dk, dv), F32)],
        ),
        out_shape=[jax.ShapeDtypeStruct((m, heads * dv), BF16),
                   jax.ShapeDtypeStruct((k_out, d_out), BF16)],
        compiler_params=_cparams(("parallel", "parallel", "arbitrary")),
        name="retention",
    )(log_gamma, proj, proj, proj, proj, w_out)


def kernel(x, fox_w_in, fox_b_f, fox_q_gain, fox_k_gain, fox_w_out, ret_w_in, ret_w_out,
           ln_gain, ln_bias):
    batch, seq, d_model = x.shape
    depth = ln_gain.shape[0]
    m = batch * seq
    alpha = (2.0 * depth) ** 0.25

    fox_width = fox_w_out.shape[1]
    ret_v_width = ret_w_out.shape[1]
    ret_heads = ret_v_width // RET_V_DIM
    ret_qk_width = ret_heads * RET_QK_DIM

    inv_freq = ROPE_BASE ** (-jnp.arange(0, RET_QK_DIM, 2, dtype=F32) / RET_QK_DIM)
    ang = jnp.arange(seq, dtype=F32)[:, None] * inv_freq[None, :]
    k_scale = RET_QK_DIM ** -0.5
    cos2 = jnp.stack([jnp.cos(ang), jnp.cos(ang) * k_scale])
    sin2 = jnp.stack([jnp.sin(ang), jnp.sin(ang) * k_scale])
    log_gamma = jnp.log1p(-jnp.exp2(-5.0 - jnp.arange(ret_heads, dtype=F32)))

    fox_w_in_t = jnp.swapaxes(fox_w_in, 1, 2)
    h = x.reshape(m, d_model)
    h_bf = None
    for i in range(depth):
        j = i // 2
        if i % 2 == 0:
            if h_bf is None:
                c3, h_bf = _fox_forget_cumsum(h, fox_w_in_t, j, fox_b_f[j], batch, seq,
                                              4 * fox_width)
                proj = _fox_in_proj(h_bf, fox_w_in_t, j, fox_q_gain[j], fox_k_gain[j], fox_width)
            else:
                proj = _fox_in_proj(h_bf, fox_w_in_t, j, fox_q_gain[j], fox_k_gain[j], fox_width)
                c3 = _fox_forget_cumsum(h_bf, fox_w_in_t, j, fox_b_f[j], batch, seq,
                                        4 * fox_width)
            y, w_out = _fox_attention(proj, c3, fox_w_out, j, batch, seq)
        else:
            proj = _ret_in_proj(h_bf, ret_w_in, j, cos2, sin2, seq, ret_qk_width, ret_v_width)
            y, w_out = _retention(proj, log_gamma, ret_w_out, j, batch, seq, ret_heads)
        h, h_bf = _out_proj_ln(y, w_out, h, ln_gain[i], ln_bias[i], alpha)
    return h.reshape(batch, seq, d_model)
```

```python
import functools
import math

import jax
import jax.numpy as jnp
from jax import lax
from jax.experimental import pallas as pl
from jax.experimental.pallas import tpu as pltpu

F32 = jnp.float32
BF16 = jnp.bfloat16

LANES = 128
MXU_COLS = 256
IN_PROJ_ROW_CHUNK = 256
EPILOGUE_ROWS = 32

FOX_HEAD_DIM = 128
RET_QK_DIM = 256
RET_V_DIM = 512
ROPE_BASE = 10000.0
LN_EPS = 1e-5
GN_EPS = 1e-6
QK_EPS = 1e-6
LOG2E = math.log2(math.e)

VMEM_LIMIT = 56 * 1024 * 1024

NT_DIMS = (((1,), (1,)), ((), ()))
TN_DIMS = (((0,), (0,)), ((), ()))


def _cparams(sem):
    return pltpu.CompilerParams(dimension_semantics=sem, vmem_limit_bytes=VMEM_LIMIT)


def _silu(g):
    return g * (1.0 / (1.0 + jnp.exp(-g)))


def _lagged_tile(t, n_tiles):
    return jnp.minimum(t, n_tiles - 1), jnp.maximum(t, 1) - 1


def _in_proj_body(x_ref, w_ref, o_ref, acc_ref, wb_ref, mixed_epilogue, *,
                  col_tiles, mixed_tiles, plain_tiles):
    t = pl.program_id(0)
    n_tiles = pl.num_programs(0) - 1
    _, done = _lagged_tile(t, n_tiles)
    jd = done % col_tiles
    tm, tn = o_ref.shape

    def step(mm_slot, done_slot, first_and_last):
        def run(do_matmul, epilogue):
            w_src = w_ref
            if do_matmul and w_ref.dtype != BF16:
                w = w_ref[...]
                if w.shape != wb_ref.shape:
                    w = w.T
                wb_ref[...] = w.astype(BF16)
                w_src = wb_ref
            for r in range(0, tm, IN_PROJ_ROW_CHUNK):
                rows = slice(r, r + IN_PROJ_ROW_CHUNK)
                if do_matmul:
                    acc_ref[mm_slot, rows, :] = jnp.dot(x_ref[rows, :], w_src[...],
                                                        preferred_element_type=F32)
                if epilogue is not None:
                    for rr in range(r, r + IN_PROJ_ROW_CHUNK, EPILOGUE_ROWS):
                        sub = slice(rr, rr + EPILOGUE_ROWS)
                        for c in range(tn // MXU_COLS):
                            cols = slice(c * MXU_COLS, (c + 1) * MXU_COLS)
                            col = c * MXU_COLS
                            for piece in epilogue(acc_ref[done_slot, sub, cols], sub):
                                o_ref[sub, col:col + piece.shape[1]] = piece.astype(o_ref.dtype)
                                col += piece.shape[1]

        running = (t > 0) & (t < n_tiles)
        kinds = [(jd < mixed_tiles, mixed_epilogue),
                 ((jd >= mixed_tiles) & (jd < plain_tiles), lambda acc, rows: [acc]),
                 (jd >= plain_tiles, lambda acc, rows: [_silu(acc)])]
        for cond, epilogue in kinds:
            pl.when(running & cond)(functools.partial(run, True, epilogue))
        if first_and_last:
            pl.when(t == 0)(functools.partial(run, True, None))
            pl.when(t == n_tiles)(functools.partial(run, False, kinds[-1][1]))

    @pl.when(t % 2 == 0)
    def _():
        step(0, 1, True)

    @pl.when(t % 2 == 1)
    def _():
        step(1, 0, False)


def _fox_in_kernel(x_ref, w_ref, g_ref, o_ref, acc_ref, wb_ref, **tiles):
    def rms(acc, rows):
        outs = []
        for hh in range(MXU_COLS // FOX_HEAD_DIM):
            a = acc[:, hh * FOX_HEAD_DIM:(hh + 1) * FOX_HEAD_DIM]
            ms = jnp.mean(a * a, axis=-1, keepdims=True)
            outs.append(a * lax.rsqrt(ms + QK_EPS) * g_ref[...])
        return outs

    _in_proj_body(x_ref, w_ref, o_ref, acc_ref, wb_ref, rms, **tiles)


def _in_proj_call(kern, name, x_bf, w_in, layer, aux, aux_specs, n, col_tiles, tm, tn,
                  w_cols_leading=False):
    m, d = x_bf.shape
    n_tiles = (m // tm) * col_tiles
    assert n_tiles % 2 == 0 and n == col_tiles * tn

    def mm_tile(t):
        return _lagged_tile(t, n_tiles)[0]

    def done_tile(t):
        return _lagged_tile(t, n_tiles)[1]

    return pl.pallas_call(
        kern,
        grid=(n_tiles + 1,),
        in_specs=[
            pl.BlockSpec((tm, d), lambda t: (mm_tile(t) // col_tiles, 0)),
            (pl.BlockSpec((None, tn, d), lambda t: (layer, mm_tile(t) % col_tiles, 0))
             if w_cols_leading else
             pl.BlockSpec((None, d, tn), lambda t: (layer, 0, mm_tile(t) % col_tiles))),
        ] + [pl.BlockSpec(shape, functools.partial(lambda t, f: f(done_tile(t)), f=f))
             for shape, f in aux_specs],
        out_specs=pl.BlockSpec((tm, tn), lambda t: (done_tile(t) // col_tiles,
                                                    done_tile(t) % col_tiles)),
        out_shape=jax.ShapeDtypeStruct((m, n), BF16),
        scratch_shapes=[
            pltpu.VMEM((2, tm, tn), F32),
            pltpu.VMEM((d, tn) if w_in.dtype != BF16 else (16, LANES), BF16),
        ],
        compiler_params=_cparams(("arbitrary",)),
        name=name,
    )(x_bf, w_in, *aux)


def _fox_in_proj(x_bf, w_in_t, layer, q_gain, k_gain, width, *, tm=2048, tn=512):
    col_tiles = 4 * width // tn
    per_region = width // tn
    gains = jnp.stack([q_gain * (FOX_HEAD_DIM ** -0.5 * LOG2E), k_gain]).reshape(2, 1, -1)
    kern = functools.partial(_fox_in_kernel, col_tiles=col_tiles, mixed_tiles=2 * per_region,
                             plain_tiles=3 * per_region)
    gain_spec = ((None, 1, FOX_HEAD_DIM),
                 lambda done: (jnp.minimum((done % col_tiles) // per_region, 1), 0, 0))
    return _in_proj_call(kern, "fox_in_proj", x_bf, w_in_t, layer, [gains], [gain_spec],
                         4 * width, col_tiles, tm, tn, w_cols_leading=True)


def _ret_in_kernel(x_ref, w_ref, cos_ref, sin_ref, o_ref, acc_ref, wb_ref, **tiles):
    half = RET_QK_DIM // 2

    def rope(acc, rows):
        cos = cos_ref[rows, :]
        sin = sin_ref[rows, :]
        t1 = acc[:, :half]
        t2 = acc[:, half:]
        return [t1 * cos - t2 * sin, t1 * sin + t2 * cos]

    _in_proj_body(x_ref, w_ref, o_ref, acc_ref, wb_ref, rope, **tiles)


def _ret_in_proj(x_bf, w_in, layer, cos2, sin2, seq, qk_width, v_width, *, tm=2048, tn=512):
    assert RET_QK_DIM == MXU_COLS
    n = w_in.shape[2]
    col_tiles = n // tn
    q_tiles = qk_width // tn
    half = RET_QK_DIM // 2
    pos_blocks = seq // tm
    kern = functools.partial(_ret_in_kernel, col_tiles=col_tiles, mixed_tiles=2 * q_tiles,
                             plain_tiles=(2 * qk_width + v_width) // tn)
    table_spec = ((None, tm, half),
                  lambda done: (jnp.minimum((done % col_tiles) // q_tiles, 1),
                                (done // col_tiles) % pos_blocks, 0))
    return _in_proj_call(kern, "ret_in_proj", x_bf, w_in, layer, [cos2, sin2],
                         [table_spec, table_spec], n, col_tiles, tm, tn)


N_SPLIT = 3


def _split3(v):
    hi = v.astype(BF16)
    r = v - hi.astype(F32)
    mid = r.astype(BF16)
    lo = (r - mid.astype(F32)).astype(BF16)
    return hi, mid, lo


def _fgate_kernel(x_ref, w_ref, b_ref, c3_ref, *rest):
    carry_ref = rest[-1]
    ts = x_ref.shape[0]
    n_heads = w_ref.shape[0]

    @pl.when(pl.program_id(1) == 0)
    def _():
        carry_ref[...] = jnp.zeros_like(carry_ref)

    x = x_ref[...].astype(BF16)
    if len(rest) == 2:
        rest[0][...] = x
    z = lax.dot_general(w_ref[...].astype(BF16), x, NT_DIMS,
                        preferred_element_type=F32) + b_ref[...]
    log_f = jnp.minimum(z, 0.0) - jnp.log1p(jnp.exp(-jnp.abs(z)))
    row = lax.broadcasted_iota(jnp.int32, (ts, ts), 0)
    col = lax.broadcasted_iota(jnp.int32, (ts, ts), 1)
    tri = jnp.where(row <= col, 1.0, 0.0).astype(BF16)
    cs = carry_ref[...]
    for part in _split3(log_f):
        cs = cs + jnp.dot(part, tri, preferred_element_type=F32)
    carry_ref[...] = cs[:, ts - 1:ts]
    parts = [p.astype(F32) for p in _split3(cs * LOG2E)]
    pad = jnp.zeros((LANES - N_SPLIT * n_heads, ts), F32)
    c3_ref[...] = jnp.concatenate(parts + [pad], axis=0).T.astype(c3_ref.dtype)


def _fox_forget_cumsum(x, w_in_t, layer, b_f, batch, seq, f_col0, *, ts=512):
    m, d = x.shape
    h = b_f.shape[0]
    ns = seq // ts
    assert f_col0 % h == 0 and w_in_t.shape[1] == f_col0 + h and N_SPLIT * h <= LANES
    row_block = lambda bi, si: (bi * ns + si, 0)
    out_specs = [pl.BlockSpec((ts, LANES), row_block)]
    out_shape = [jax.ShapeDtypeStruct((m, LANES), BF16)]
    if x.dtype != BF16:
        out_specs.append(pl.BlockSpec((ts, d), row_block))
        out_shape.append(jax.ShapeDtypeStruct((m, d), BF16))
    outs = pl.pallas_call(
        _fgate_kernel,
        grid=(batch, ns),
        in_specs=[
            pl.BlockSpec((ts, d), row_block),
            pl.BlockSpec((None, h, d), lambda bi, si: (layer, f_col0 // h, 0)),
            pl.BlockSpec((h, 1), lambda bi, si: (0, 0)),
        ],
        out_specs=out_specs,
        out_shape=out_shape,
        scratch_shapes=[pltpu.VMEM((h, 1), F32)],
        compiler_params=_cparams(("parallel", "arbitrary")),
        name="fox_forget_cumsum",
    )(x, w_in_t, b_f.reshape(-1, 1))
    return outs if len(outs) == 2 else outs[0]


def _fox_attn_kernel(q_ref, k_ref, v_ref, g_ref, c3_ref, wo_ref, o_ref, wob_ref,
                     qa_ref, kaug_ref, vaug_ref, m_ref, acc_ref, p_ref, alpha_ref,
                     *, blk, n_heads):
    h = pl.program_id(1)
    wob_ref[...] = wo_ref[...].astype(wob_ref.dtype)
    s_len = k_ref.shape[0]
    dh = FOX_HEAD_DIM
    n_blk = s_len // blk

    shift = LANES // 2
    lane = lax.broadcasted_iota(jnp.int32, (blk, LANES), 1)
    mine = (lane < N_SPLIT * n_heads) & (lane % n_heads == h)
    partner = (lane >= shift) & (lane - shift < N_SPLIT * n_heads) & ((lane - shift) % n_heads == h)
    for b in range(n_blk):
        rows = slice(b * blk, (b + 1) * blk)
        terms = jnp.where(mine, c3_ref[rows, :].astype(F32), 0.0)
        qa_ref[rows, :dh] = q_ref[rows, :]
        qa_ref[rows, dh:] = (terms + jnp.where(partner, 1.0, 0.0)).astype(BF16)
        kaug_ref[rows, :dh] = k_ref[rows, :]
        kaug_ref[rows, dh:] = (jnp.where(mine, 1.0, 0.0)
                               - pltpu.roll(terms, shift, axis=1)).astype(BF16)
        vaug_ref[rows, :dh] = v_ref[rows, :]
        vaug_ref[rows, dh:] = jnp.ones((blk, dh), BF16)
        m_ref[rows, :] = jnp.full((blk, LANES), -jnp.inf, F32)
        acc_ref[rows, :] = jnp.zeros((blk, 2 * dh), F32)

    def qk_softmax(slot, r0, nr, k0, nk, last):
        rows = pl.ds(r0, nr)
        s = lax.dot_general(qa_ref[rows, :], kaug_ref[pl.ds(k0, nk), :], NT_DIMS,
                            preferred_element_type=F32)
        if last:
            row = lax.broadcasted_iota(jnp.int32, (nr, nk), 0) + r0
            col = lax.broadcasted_iota(jnp.int32, (nr, nk), 1) + k0
            s = jnp.where(col <= row, s, -jnp.inf)
        m_old = m_ref[rows, :]
        m_new = jnp.maximum(m_old, jnp.max(s, axis=-1, keepdims=True))
        alpha_ref[slot, :nr, :] = jnp.exp2(m_old - m_new)
        p_ref[slot, :nr, :nk] = jnp.exp2(
            s - jnp.concatenate([m_new] * (nk // LANES), axis=1)).astype(BF16)
        m_ref[rows, :] = m_new

    def pv_update(slot, r0, nr, k0, nk, last):
        rows = pl.ds(r0, nr)
        pv = jnp.dot(p_ref[slot, :nr, :nk], vaug_ref[pl.ds(k0, nk), :],
                     preferred_element_type=F32)
        alpha = alpha_ref[slot, :nr, :]
        acc_ref[rows, :] = jnp.concatenate([alpha, alpha], axis=1) * acc_ref[rows, :] + pv
        if last:
            o_ref[rows, :] = (acc_ref[rows, :dh] / acc_ref[rows, dh:]
                              * g_ref[rows, :].astype(F32)).astype(o_ref.dtype)

    half = blk // 2
    tasks = []
    for qi in range(n_blk):
        q0 = qi * blk
        tasks += [(q0, blk, ki * blk, blk, False) for ki in range(qi)]
        tasks += [(q0, half, q0, half, True), (q0 + half, half, q0, blk, True)]
    for t, task in enumerate(tasks):
        qk_softmax(t % 2, *task)
        if t > 0:
            pv_update((t - 1) % 2, *tasks[t - 1])
    pv_update((len(tasks) - 1) % 2, *tasks[-1])


def _fox_attention(proj, c3, w_out, layer, batch, seq, *, blk=512):
    m, n = proj.shape
    k_out, d_out = w_out.shape[1:]
    w_rows = k_out // (batch * (n // 4 // FOX_HEAD_DIM))
    width = n // 4
    heads = width // FOX_HEAD_DIM
    dh = FOX_HEAD_DIM
    assert dh == LANES and c3.shape[1] == LANES and N_SPLIT * heads <= LANES // 2
    kern = functools.partial(_fox_attn_kernel, blk=blk, n_heads=heads)
    return pl.pallas_call(
        kern,
        grid=(batch, heads),
        in_specs=[
            pl.BlockSpec((seq, dh), lambda b, h: (b, h)),
            pl.BlockSpec((seq, dh), lambda b, h: (b, heads + h)),
            pl.BlockSpec((seq, dh), lambda b, h: (b, 2 * heads + h)),
            pl.BlockSpec((seq, dh), lambda b, h: (b, 3 * heads + h)),
            pl.BlockSpec((seq, LANES), lambda b, h: (b, 0)),
            pl.BlockSpec((None, w_rows, d_out), lambda b, h: (layer, b * heads + h, 0)),
        ],
        out_specs=[pl.BlockSpec((seq, dh), lambda b, h: (b, h)),
                   pl.BlockSpec((w_rows, d_out), lambda b, h: (b * heads + h, 0))],
        out_shape=[jax.ShapeDtypeStruct((m, width), BF16),
                   jax.ShapeDtypeStruct((k_out, d_out), BF16)],
        scratch_shapes=[
            pltpu.VMEM((seq, 2 * dh), BF16),
            pltpu.VMEM((seq, 2 * dh), BF16),
            pltpu.VMEM((seq, 2 * dh), BF16),
            pltpu.VMEM((seq, LANES), F32),
            pltpu.VMEM((seq, 2 * dh), F32),
            pltpu.VMEM((2, blk, blk), BF16),
            pltpu.VMEM((2, blk, LANES), F32),
        ],
        compiler_params=_cparams(("parallel", "parallel")),
        name="fox_attention",
    )(proj, proj, proj, proj, c3, w_out)


def _out_ln_kernel(y_ref, w_ref, h_ref, g_ref, b_ref, o_ref, obf_ref, *, alpha, row_group):
    for r in range(0, y_ref.shape[0], row_group):
        rows = slice(r, r + row_group)
        br = jnp.dot(y_ref[rows, :], w_ref[...], preferred_element_type=F32)
        z = alpha * h_ref[rows, :] + br
        mu = jnp.mean(z, axis=-1, keepdims=True)
        zc = z - mu
        var = jnp.mean(zc * zc, axis=-1, keepdims=True)
        out = zc * lax.rsqrt(var + LN_EPS) * g_ref[...] + b_ref[...]
        o_ref[rows, :] = out
        obf_ref[rows, :] = out.astype(obf_ref.dtype)


def _out_proj_ln(y_bf, w_bf, h, gain, bias, alpha, *, tm=512, row_group=256):
    m, k = y_bf.shape
    d = w_bf.shape[1]
    kern = functools.partial(_out_ln_kernel, alpha=alpha, row_group=row_group)
    return pl.pallas_call(
        kern,
        grid=(m // tm,),
        in_specs=[
            pl.BlockSpec((tm, k), lambda i: (i, 0)),
            pl.BlockSpec((k, d), lambda i: (0, 0), pipeline_mode=pl.Buffered(1)),
            pl.BlockSpec((tm, d), lambda i: (i, 0)),
            pl.BlockSpec((1, d), lambda i: (0, 0)),
            pl.BlockSpec((1, d), lambda i: (0, 0)),
        ],
        out_specs=[pl.BlockSpec((tm, d), lambda i: (i, 0)),
                   pl.BlockSpec((tm, d), lambda i: (i, 0))],
        out_shape=[jax.ShapeDtypeStruct((m, d), F32), jax.ShapeDtypeStruct((m, d), BF16)],
        compiler_params=_cparams(("parallel",)),
        name="out_proj_ln",
    )(y_bf, w_bf, h, gain.reshape(1, -1), bias.reshape(1, -1))


def _retention_kernel(lg_ref, q_ref, k_ref, v_ref, g_ref, wo_ref, o_ref, wob_ref, r_ref, *, chunk):
    h = pl.program_id(1)
    n = pl.program_id(2)
    wob_ref[...] = wo_ref[...].astype(wob_ref.dtype)

    @pl.when(n == 0)
    def _():
        r_ref[...] = jnp.zeros_like(r_ref)

    lg = lg_ref[h]
    row = lax.broadcasted_iota(jnp.int32, (chunk, chunk), 0)
    col = lax.broadcasted_iota(jnp.int32, (chunk, chunk), 1)
    diff = (row - col).astype(F32)
    intra = jnp.where(diff >= 0, jnp.exp(jnp.maximum(diff, 0.0) * lg), 0.0)
    pos = lax.broadcasted_iota(jnp.int32, (chunk, 1), 0).astype(F32)
    q_decay = jnp.exp((pos + 1.0) * lg)
    k_decay = jnp.exp((chunk - 1.0 - pos) * lg)
    c_decay = jnp.exp(jnp.full((1, 1), chunk, F32) * lg)

    r = r_ref[...]
    for ci in range(q_ref.shape[0] // chunk):
        rows = slice(ci * chunk, (ci + 1) * chunk)
        q = q_ref[rows, :]
        k = k_ref[rows, :]
        v = v_ref[rows, :]
        inner = lax.dot_general(q, k, NT_DIMS, preferred_element_type=F32) * intra
        o = (jnp.dot(inner.astype(BF16), v, preferred_element_type=F32)
             + jnp.dot(q, r.astype(BF16), preferred_element_type=F32) * q_decay)
        kd = (k.astype(F32) * k_decay).astype(BF16)
        r = r * c_decay + lax.dot_general(kd, v, TN_DIMS, preferred_element_type=F32)

        mu = jnp.mean(o, axis=-1, keepdims=True)
        oc = o - mu
        var = jnp.mean(oc * oc, axis=-1, keepdims=True)
        o_ref[rows, :] = (oc * lax.rsqrt(var + GN_EPS)
                          * g_ref[rows, :].astype(F32)).astype(o_ref.dtype)
    r_ref[...] = r


def _retention(proj, log_gamma, w_out, layer, batch, seq, heads, *, chunk=256,
               chunks_per_step=16):
    m = proj.shape[0]
    ts = chunk * chunks_per_step
    ns = seq // ts
    dk, dv = RET_QK_DIM, RET_V_DIM
    v_off = 2 * heads * dk // dv
    g_off = v_off + heads
    k_out, d_out = w_out.shape[1:]
    w_rows = k_out // (batch * heads * ns)
    step = lambda b, h, n: (b * heads + h) * ns + n
    kern = functools.partial(_retention_kernel, chunk=chunk)
    return pl.pallas_call(
        kern,
        grid_spec=pltpu.PrefetchScalarGridSpec(
            num_scalar_prefetch=1,
            grid=(batch, heads, ns),
            in_specs=[
                pl.BlockSpec((ts, dk), lambda b, h, n, lg: (b * ns + n, h)),
                pl.BlockSpec((ts, dk), lambda b, h, n, lg: (b * ns + n, heads + h)),
                pl.BlockSpec((ts, dv), lambda b, h, n, lg: (b * ns + n, v_off + h)),
                pl.BlockSpec((ts, dv), lambda b, h, n, lg: (b * ns + n, g_off + h)),
                pl.BlockSpec((None, w_rows, d_out), lambda b, h, n, lg: (layer, step(b, h, n), 0)),
            ],
            out_specs=[pl.BlockSpec((ts, dv), lambda b, h, n, lg: (b * ns + n, h)),
                       pl.BlockSpec((w_rows, d_out), lambda b, h, n, lg: (step(b, h, n), 0))],
            scratch_shapes=[pltpu.VMEM((dk, dv), F32)],
        ),
        out_shape=[jax.ShapeDtypeStruct((m, heads * dv), BF16),
                   jax.ShapeDtypeStruct((k_out, d_out), BF16)],
        compiler_params=_cparams(("parallel", "parallel", "arbitrary")),
        name="retention",
    )(log_gamma, proj, proj, proj, proj, w_out)


def kernel(x, fox_w_in, fox_b_f, fox_q_gain, fox_k_gain, fox_w_out, ret_w_in, ret_w_out,
           ln_gain, ln_bias):
    batch, seq, d_model = x.shape
    depth = ln_gain.shape[0]
    m = batch * seq
    alpha = (2.0 * depth) ** 0.25

    fox_width = fox_w_out.shape[1]
    ret_v_width = ret_w_out.shape[1]
    ret_heads = ret_v_width // RET_V_DIM
    ret_qk_width = ret_heads * RET_QK_DIM

    inv_freq = ROPE_BASE ** (-jnp.arange(0, RET_QK_DIM, 2, dtype=F32) / RET_QK_DIM)
    ang = jnp.arange(seq, dtype=F32)[:, None] * inv_freq[None, :]
    k_scale = RET_QK_DIM ** -0.5
    cos2 = jnp.stack([jnp.cos(ang), jnp.cos(ang) * k_scale])
    sin2 = jnp.stack([jnp.sin(ang), jnp.sin(ang) * k_scale])
    log_gamma = jnp.log1p(-jnp.exp2(-5.0 - jnp.arange(ret_heads, dtype=F32)))

    fox_w_in_t = jnp.swapaxes(fox_w_in, 1, 2)
    h = x.reshape(m, d_model)
    h_bf = None
    for i in range(depth):
        j = i // 2
        if i % 2 == 0:
            if h_bf is None:
                c3, h_bf = _fox_forget_cumsum(h, fox_w_in_t, j, fox_b_f[j], batch, seq,
                                              4 * fox_width)
                proj = _fox_in_proj(h_bf, fox_w_in_t, j, fox_q_gain[j], fox_k_gain[j], fox_width)
            else:
                proj = _fox_in_proj(h_bf, fox_w_in_t, j, fox_q_gain[j], fox_k_gain[j], fox_width)
                c3 = _fox_forget_cumsum(h_bf, fox_w_in_t, j, fox_b_f[j], batch, seq,
                                        4 * fox_width)
            y, w_out = _fox_attention(proj, c3, fox_w_out, j, batch, seq)
        else:
            proj = _ret_in_proj(h_bf, ret_w_in, j, cos2, sin2, seq, ret_qk_width, ret_v_width)
            y, w_out = _retention(proj, log_gamma, ret_w_out, j, batch, seq, ret_heads)
        h, h_bf = _out_proj_ln(y, w_out, h, ln_gain[i], ln_bias[i], alpha)
    return h.reshape(batch, seq, d_model)
```

```python
import functools
import math

import jax
import jax.numpy as jnp
from jax import lax
from jax.experimental import pallas as pl
from jax.experimental.pallas import tpu as pltpu

F32 = jnp.float32
BF16 = jnp.bfloat16

LANES = 128
MXU_COLS = 256
IN_PROJ_ROW_CHUNK = 256
EPILOGUE_ROWS = 32

FOX_HEAD_DIM = 128
RET_QK_DIM = 256
RET_V_DIM = 512
ROPE_BASE = 10000.0
LN_EPS = 1e-5
GN_EPS = 1e-6
QK_EPS = 1e-6
LOG2E = math.log2(math.e)

VMEM_LIMIT = 56 * 1024 * 1024

NT_DIMS = (((1,), (1,)), ((), ()))
TN_DIMS = (((0,), (0,)), ((), ()))


def _cparams(sem):
    return pltpu.CompilerParams(dimension_semantics=sem, vmem_limit_bytes=VMEM_LIMIT)


def _silu(g):
    return g * (1.0 / (1.0 + jnp.exp(-g)))


def _lagged_tile(t, n_tiles):
    return jnp.minimum(t, n_tiles - 1), jnp.maximum(t, 1) - 1


def _in_proj_body(x_ref, w_ref, o_ref, acc_ref, wb_ref, mixed_epilogue, *,
                  col_tiles, mixed_tiles, plain_tiles):
    t = pl.program_id(0)
    n_tiles = pl.num_programs(0) - 1
    _, done = _lagged_tile(t, n_tiles)
    jd = done % col_tiles
    tm, tn = o_ref.shape

    def step(mm_slot, done_slot, first_and_last):
        def run(do_matmul, epilogue):
            w_src = w_ref
            if do_matmul and w_ref.dtype != BF16:
                w = w_ref[...]
                if w.shape != wb_ref.shape:
                    w = w.T
                wb_ref[...] = w.astype(BF16)
                w_src = wb_ref
            for r in range(0, tm, IN_PROJ_ROW_CHUNK):
                rows = slice(r, r + IN_PROJ_ROW_CHUNK)
                if do_matmul:
                    acc_ref[mm_slot, rows, :] = jnp.dot(x_ref[rows, :], w_src[...],
                                                        preferred_element_type=F32)
                if epilogue is not None:
                    for rr in range(r, r + IN_PROJ_ROW_CHUNK, EPILOGUE_ROWS):
                        sub = slice(rr, rr + EPILOGUE_ROWS)
                        for c in range(tn // MXU_COLS):
                            cols = slice(c * MXU_COLS, (c + 1) * MXU_COLS)
                            col = c * MXU_COLS
                            for piece in epilogue(acc_ref[done_slot, sub, cols], sub):
                                o_ref[sub, col:col + piece.shape[1]] = piece.astype(o_ref.dtype)
                                col += piece.shape[1]

        running = (t > 0) & (t < n_tiles)
        kinds = [(jd < mixed_tiles, mixed_epilogue),
                 ((jd >= mixed_tiles) & (jd < plain_tiles), lambda acc, rows: [acc]),
                 (jd >= plain_tiles, lambda acc, rows: [_silu(acc)])]
        for cond, epilogue in kinds:
            pl.when(running & cond)(functools.partial(run, True, epilogue))
        if first_and_last:
            pl.when(t == 0)(functools.partial(run, True, None))
            pl.when(t == n_tiles)(functools.partial(run, False, kinds[-1][1]))

    @pl.when(t % 2 == 0)
    def _():
        step(0, 1, True)

    @pl.when(t % 2 == 1)
    def _():
        step(1, 0, False)


def _fox_in_kernel(x_ref, w_ref, g_ref, o_ref, acc_ref, wb_ref, **tiles):
    def rms(acc, rows):
        outs = []
        for hh in range(MXU_COLS // FOX_HEAD_DIM):
            a = acc[:, hh * FOX_HEAD_DIM:(hh + 1) * FOX_HEAD_DIM]
            ms = jnp.mean(a * a, axis=-1, keepdims=True)
            outs.append(a * lax.rsqrt(ms + QK_EPS) * g_ref[...])
        return outs

    _in_proj_body(x_ref, w_ref, o_ref, acc_ref, wb_ref, rms, **tiles)


def _in_proj_call(kern, name, x_bf, w_in, layer, aux, aux_specs, n, col_tiles, tm, tn,
                  w_cols_leading=False):
    m, d = x_bf.shape
    n_tiles = (m // tm) * col_tiles
    assert n_tiles % 2 == 0 and n == col_tiles * tn

    def mm_tile(t):
        return _lagged_tile(t, n_tiles)[0]

    def done_tile(t):
        return _lagged_tile(t, n_tiles)[1]

    return pl.pallas_call(
        kern,
        grid=(n_tiles + 1,),
        in_specs=[
            pl.BlockSpec((tm, d), lambda t: (mm_tile(t) // col_tiles, 0)),
            (pl.BlockSpec((None, tn, d), lambda t: (layer, mm_tile(t) % col_tiles, 0))
             if w_cols_leading else
             pl.BlockSpec((None, d, tn), lambda t: (layer, 0, mm_tile(t) % col_tiles))),
        ] + [pl.BlockSpec(shape, functools.partial(lambda t, f: f(done_tile(t)), f=f))
             for shape, f in aux_specs],
        out_specs=pl.BlockSpec((tm, tn), lambda t: (done_tile(t) // col_tiles,
                                                    done_tile(t) % col_tiles)),
        out_shape=jax.ShapeDtypeStruct((m, n), BF16),
        scratch_shapes=[
            pltpu.VMEM((2, tm, tn), F32),
            pltpu.VMEM((d, tn) if w_in.dtype != BF16 else (16, LANES), BF16),
        ],
        compiler_params=_cparams(("arbitrary",)),
        name=name,
    )(x_bf, w_in, *aux)


def _fox_in_proj(x_bf, w_in_t, layer, q_gain, k_gain, width, *, tm=2048, tn=512):
    col_tiles = 4 * width // tn
    per_region = width // tn
    gains = jnp.stack([q_gain * (FOX_HEAD_DIM ** -0.5 * LOG2E), k_gain]).reshape(2, 1, -1)
    kern = functools.partial(_fox_in_kernel, col_tiles=col_tiles, mixed_tiles=2 * per_region,
                             plain_tiles=3 * per_region)
    gain_spec = ((None, 1, FOX_HEAD_DIM),
                 lambda done: (jnp.minimum((done % col_tiles) // per_region, 1), 0, 0))
    return _in_proj_call(kern, "fox_in_proj", x_bf, w_in_t, layer, [gains], [gain_spec],
                         4 * width, col_tiles, tm, tn, w_cols_leading=True)


def _ret_in_kernel(x_ref, w_ref, cos_ref, sin_ref, o_ref, acc_ref, wb_ref, **tiles):
    half = RET_QK_DIM // 2

    def rope(acc, rows):
        cos = cos_ref[rows, :]
        sin = sin_ref[rows, :]
        t1 = acc[:, :half]
        t2 = acc[:, half:]
        return [t1 * cos - t2 * sin, t1 * sin + t2 * cos]

    _in_proj_body(x_ref, w_ref, o_ref, acc_ref, wb_ref, rope, **tiles)


def _ret_in_proj(x_bf, w_in, layer, cos2, sin2, seq, qk_width, v_width, *, tm=2048, tn=512):
    assert RET_QK_DIM == MXU_COLS
    n = w_in.shape[2]
    col_tiles = n // tn
    q_tiles = qk_width // tn
    half = RET_QK_DIM // 2
    pos_blocks = seq // tm
    kern = functools.partial(_ret_in_kernel, col_tiles=col_tiles, mixed_tiles=2 * q_tiles,
                             plain_tiles=(2 * qk_width + v_width) // tn)
    table_spec = ((None, tm, half),
                  lambda done: (jnp.minimum((done % col_tiles) // q_tiles, 1),
                                (done // col_tiles) % pos_blocks, 0))
    return _in_proj_call(kern, "ret_in_proj", x_bf, w_in, layer, [cos2, sin2],
                         [table_spec, table_spec], n, col_tiles, tm, tn)


N_SPLIT = 3
KEY_BLOCKS_PER_TASK = 2


def _split3(v):
    hi = v.astype(BF16)
    r = v - hi.astype(F32)
    mid = r.astype(BF16)
    lo = (r - mid.astype(F32)).astype(BF16)
    return hi, mid, lo


def _fgate_kernel(x_ref, w_ref, b_ref, c3_ref, *rest):
    carry_ref = rest[-1]
    ts = x_ref.shape[0]
    n_heads = w_ref.shape[0]

    @pl.when(pl.program_id(1) == 0)
    def _():
        carry_ref[...] = jnp.zeros_like(carry_ref)

    x = x_ref[...].astype(BF16)
    if len(rest) == 2:
        rest[0][...] = x
    z = lax.dot_general(w_ref[...].astype(BF16), x, NT_DIMS,
                        preferred_element_type=F32) + b_ref[...]
    log_f = jnp.minimum(z, 0.0) - jnp.log1p(jnp.exp(-jnp.abs(z)))
    row = lax.broadcasted_iota(jnp.int32, (ts, ts), 0)
    col = lax.broadcasted_iota(jnp.int32, (ts, ts), 1)
    tri = jnp.where(row <= col, 1.0, 0.0).astype(BF16)
    cs = carry_ref[...]
    for part in _split3(log_f):
        cs = cs + jnp.dot(part, tri, preferred_element_type=F32)
    carry_ref[...] = cs[:, ts - 1:ts]
    parts = [p.astype(F32) for p in _split3(cs * LOG2E)]
    pad = jnp.zeros((LANES - N_SPLIT * n_heads, ts), F32)
    c3_ref[...] = jnp.concatenate(parts + [pad], axis=0).T.astype(c3_ref.dtype)


def _fox_forget_cumsum(x, w_in_t, layer, b_f, batch, seq, f_col0, *, ts=512):
    m, d = x.shape
    h = b_f.shape[0]
    ns = seq // ts
    assert f_col0 % h == 0 and w_in_t.shape[1] == f_col0 + h and N_SPLIT * h <= LANES
    row_block = lambda bi, si: (bi * ns + si, 0)
    out_specs = [pl.BlockSpec((ts, LANES), row_block)]
    out_shape = [jax.ShapeDtypeStruct((m, LANES), BF16)]
    if x.dtype != BF16:
        out_specs.append(pl.BlockSpec((ts, d), row_block))
        out_shape.append(jax.ShapeDtypeStruct((m, d), BF16))
    outs = pl.pallas_call(
        _fgate_kernel,
        grid=(batch, ns),
        in_specs=[
            pl.BlockSpec((ts, d), row_block),
            pl.BlockSpec((None, h, d), lambda bi, si: (layer, f_col0 // h, 0)),
            pl.BlockSpec((h, 1), lambda bi, si: (0, 0)),
        ],
        out_specs=out_specs,
        out_shape=out_shape,
        scratch_shapes=[pltpu.VMEM((h, 1), F32)],
        compiler_params=_cparams(("parallel", "arbitrary")),
        name="fox_forget_cumsum",
    )(x, w_in_t, b_f.reshape(-1, 1))
    return outs if len(outs) == 2 else outs[0]


def _fox_attn_kernel(q_ref, k_ref, v_ref, g_ref, c3_ref, wo_ref, o_ref, wob_ref,
                     qa_ref, kaug_ref, vaug_ref, m_ref, acc_ref, p_ref, alpha_ref,
                     *, blk, n_heads):
    h = pl.program_id(1)
    wob_ref[...] = wo_ref[...].astype(wob_ref.dtype)
    s_len = k_ref.shape[0]
    dh = FOX_HEAD_DIM
    n_blk = s_len // blk

    shift = LANES // 2
    lane = lax.broadcasted_iota(jnp.int32, (blk, LANES), 1)
    mine = (lane < N_SPLIT * n_heads) & (lane % n_heads == h)
    partner = (lane >= shift) & (lane - shift < N_SPLIT * n_heads) & ((lane - shift) % n_heads == h)
    for b in range(n_blk):
        rows = slice(b * blk, (b + 1) * blk)
        terms = jnp.where(mine, c3_ref[rows, :].astype(F32), 0.0)
        qa_ref[rows, :dh] = q_ref[rows, :]
        qa_ref[rows, dh:] = (terms + jnp.where(partner, 1.0, 0.0)).astype(BF16)
        kaug_ref[rows, :dh] = k_ref[rows, :]
        kaug_ref[rows, dh:] = (jnp.where(mine, 1.0, 0.0)
                               - pltpu.roll(terms, shift, axis=1)).astype(BF16)
        vaug_ref[rows, :dh] = v_ref[rows, :]
        vaug_ref[rows, dh:] = jnp.ones((blk, dh), BF16)
        m_ref[rows, :] = jnp.full((blk, LANES), -jnp.inf, F32)
        acc_ref[rows, :] = jnp.zeros((blk, 2 * dh), F32)

    def qk_softmax(slot, r0, nr, k0, nk, last):
        rows = pl.ds(r0, nr)
        s = lax.dot_general(qa_ref[rows, :], kaug_ref[pl.ds(k0, nk), :], NT_DIMS,
                            preferred_element_type=F32)
        if last:
            row = lax.broadcasted_iota(jnp.int32, (nr, nk), 0) + r0
            col = lax.broadcasted_iota(jnp.int32, (nr, nk), 1) + k0
            s = jnp.where(col <= row, s, -jnp.inf)
        m_old = m_ref[rows, :]
        m_new = jnp.maximum(m_old, jnp.max(s, axis=-1, keepdims=True))
        alpha_ref[slot, :nr, :] = jnp.exp2(m_old - m_new)
        p_ref[slot, :nr, :nk] = jnp.exp2(
            s - jnp.concatenate([m_new] * (nk // LANES), axis=1)).astype(BF16)
        m_ref[rows, :] = m_new

    def pv_update(slot, r0, nr, k0, nk, last):
        rows = pl.ds(r0, nr)
        pv = jnp.dot(p_ref[slot, :nr, :nk], vaug_ref[pl.ds(k0, nk), :],
                     preferred_element_type=F32)
        alpha = alpha_ref[slot, :nr, :]
        acc_ref[rows, :] = jnp.concatenate([alpha, alpha], axis=1) * acc_ref[rows, :] + pv
        if last:
            o_ref[rows, :] = (acc_ref[rows, :dh] / acc_ref[rows, dh:]
                              * g_ref[rows, :].astype(F32)).astype(o_ref.dtype)

    half = blk // 2
    tasks = []
    for qi in range(n_blk):
        q0 = qi * blk
        tasks += [(q0, blk, ki * blk, min(KEY_BLOCKS_PER_TASK, qi - ki) * blk, False)
                  for ki in range(0, qi, KEY_BLOCKS_PER_TASK)]
        tasks += [(q0, half, q0, half, True), (q0 + half, half, q0, blk, True)]
    for t, task in enumerate(tasks):
        qk_softmax(t % 2, *task)
        if t > 0:
            pv_update((t - 1) % 2, *tasks[t - 1])
    pv_update((len(tasks) - 1) % 2, *tasks[-1])


def _fox_attention(proj, c3, w_out, layer, batch, seq, *, blk=512):
    m, n = proj.shape
    k_out, d_out = w_out.shape[1:]
    w_rows = k_out // (batch * (n // 4 // FOX_HEAD_DIM))
    width = n // 4
    heads = width // FOX_HEAD_DIM
    dh = FOX_HEAD_DIM
    assert dh == LANES and c3.shape[1] == LANES and N_SPLIT * heads <= LANES // 2
    kern = functools.partial(_fox_attn_kernel, blk=blk, n_heads=heads)
    return pl.pallas_call(
        kern,
        grid=(batch, heads),
        in_specs=[
            pl.BlockSpec((seq, dh), lambda b, h: (b, h)),
            pl.BlockSpec((seq, dh), lambda b, h: (b, heads + h)),
            pl.BlockSpec((seq, dh), lambda b, h: (b, 2 * heads + h)),
            pl.BlockSpec((seq, dh), lambda b, h: (b, 3 * heads + h)),
            pl.BlockSpec((seq, LANES), lambda b, h: (b, 0)),
            pl.BlockSpec((None, w_rows, d_out), lambda b, h: (layer, b * heads + h, 0)),
        ],
        out_specs=[pl.BlockSpec((seq, dh), lambda b, h: (b, h)),
                   pl.BlockSpec((w_rows, d_out), lambda b, h: (b * heads + h, 0))],
        out_shape=[jax.ShapeDtypeStruct((m, width), BF16),
                   jax.ShapeDtypeStruct((k_out, d_out), BF16)],
        scratch_shapes=[
            pltpu.VMEM((seq, 2 * dh), BF16),
            pltpu.VMEM((seq, 2 * dh), BF16),
            pltpu.VMEM((seq, 2 * dh), BF16),
            pltpu.VMEM((seq, LANES), F32),
            pltpu.VMEM((seq, 2 * dh), F32),
            pltpu.VMEM((2, blk, KEY_BLOCKS_PER_TASK * blk), BF16),
            pltpu.VMEM((2, blk, LANES), F32),
        ],
        compiler_params=_cparams(("parallel", "parallel")),
        name="fox_attention",
    )(proj, proj, proj, proj, c3, w_out)


def _out_ln_kernel(y_ref, w_ref, h_ref, g_ref, b_ref, o_ref, obf_ref, *, alpha, row_group):
    for r in range(0, y_ref.shape[0], row_group):
        rows = slice(r, r + row_group)
        br = jnp.dot(y_ref[rows, :], w_ref[...], preferred_element_type=F32)
        z = alpha * h_ref[rows, :] + br
        mu = jnp.mean(z, axis=-1, keepdims=True)
        zc = z - mu
        var = jnp.mean(zc * zc, axis=-1, keepdims=True)
        out = zc * lax.rsqrt(var + LN_EPS) * g_ref[...] + b_ref[...]
        o_ref[rows, :] = out
        obf_ref[rows, :] = out.astype(obf_ref.dtype)


def _out_proj_ln(y_bf, w_bf, h, gain, bias, alpha, *, tm=512, row_group=256):
    m, k = y_bf.shape
    d = w_bf.shape[1]
    kern = functools.partial(_out_ln_kernel, alpha=alpha, row_group=row_group)
    return pl.pallas_call(
        kern,
        grid=(m // tm,),
        in_specs=[
            pl.BlockSpec((tm, k), lambda i: (i, 0)),
            pl.BlockSpec((k, d), lambda i: (0, 0), pipeline_mode=pl.Buffered(1)),
            pl.BlockSpec((tm, d), lambda i: (i, 0)),
            pl.BlockSpec((1, d), lambda i: (0, 0)),
            pl.BlockSpec((1, d), lambda i: (0, 0)),
        ],
        out_specs=[pl.BlockSpec((tm, d), lambda i: (i, 0)),
                   pl.BlockSpec((tm, d), lambda i: (i, 0))],
        out_shape=[jax.ShapeDtypeStruct((m, d), F32), jax.ShapeDtypeStruct((m, d), BF16)],
        compiler_params=_cparams(("parallel",)),
        name="out_proj_ln",
    )(y_bf, w_bf, h, gain.reshape(1, -1), bias.reshape(1, -1))


def _retention_kernel(lg_ref, q_ref, k_ref, v_ref, g_ref, wo_ref, o_ref, wob_ref, r_ref, *, chunk):
    h = pl.program_id(1)
    n = pl.program_id(2)
    wob_ref[...] = wo_ref[...].astype(wob_ref.dtype)

    @pl.when(n == 0)
    def _():
        r_ref[...] = jnp.zeros_like(r_ref)

    lg = lg_ref[h]
    row = lax.broadcasted_iota(jnp.int32, (chunk, chunk), 0)
    col = lax.broadcasted_iota(jnp.int32, (chunk, chunk), 1)
    diff = (row - col).astype(F32)
    intra = jnp.where(diff >= 0, jnp.exp(jnp.maximum(diff, 0.0) * lg), 0.0)
    pos = lax.broadcasted_iota(jnp.int32, (chunk, 1), 0).astype(F32)
    q_decay = jnp.exp((pos + 1.0) * lg)
    k_decay = jnp.exp((chunk - 1.0 - pos) * lg)
    c_decay = jnp.exp(jnp.full((1, 1), chunk, F32) * lg)

    r = r_ref[...]
    for ci in range(q_ref.shape[0] // chunk):
        rows = slice(ci * chunk, (ci + 1) * chunk)
        q = q_ref[rows, :]
        k = k_ref[rows, :]
        v = v_ref[rows, :]
        inner = lax.dot_general(q, k, NT_DIMS, preferred_element_type=F32) * intra
        o = (jnp.dot(inner.astype(BF16), v, preferred_element_type=F32)
             + jnp.dot(q, r.astype(BF16), preferred_element_type=F32) * q_decay)
        kd = (k.astype(F32) * k_decay).astype(BF16)
        r = r * c_decay + lax.dot_general(kd, v, TN_DIMS, preferred_element_type=F32)

        mu = jnp.mean(o, axis=-1, keepdims=True)
        oc = o - mu
        var = jnp.mean(oc * oc, axis=-1, keepdims=True)
        o_ref[rows, :] = (oc * lax.rsqrt(var + GN_EPS)
                          * g_ref[rows, :].astype(F32)).astype(o_ref.dtype)
    r_ref[...] = r


def _retention(proj, log_gamma, w_out, layer, batch, seq, heads, *, chunk=256,
               chunks_per_step=16):
    m = proj.shape[0]
    ts = chunk * chunks_per_step
    ns = seq // ts
    dk, dv = RET_QK_DIM, RET_V_DIM
    v_off = 2 * heads * dk // dv
    g_off = v_off + heads
    k_out, d_out = w_out.shape[1:]
    w_rows = k_out // (batch * heads * ns)
    step = lambda b, h, n: (b * heads + h) * ns + n
    kern = functools.partial(_retention_kernel, chunk=chunk)
    return pl.pallas_call(
        kern,
        grid_spec=pltpu.PrefetchScalarGridSpec(
            num_scalar_prefetch=1,
            grid=(batch, heads, ns),
            in_specs=[
                pl.BlockSpec((ts, dk), lambda b, h, n, lg: (b * ns + n, h)),
                pl.BlockSpec((ts, dk), lambda b, h, n, lg: (b * ns + n, heads + h)),
                pl.BlockSpec((ts, dv), lambda b, h, n, lg: (b * ns + n, v_off + h)),
                pl.BlockSpec((ts, dv), lambda b, h, n, lg: (b * ns + n, g_off + h)),
                pl.BlockSpec((None, w_rows, d_out), lambda b, h, n, lg: (layer, step(b, h, n), 0)),
            ],
            out_specs=[pl.BlockSpec((ts, dv), lambda b, h, n, lg: (b * ns + n, h)),
                       pl.BlockSpec((w_rows, d_out), lambda b, h, n, lg: (step(b, h, n), 0))],
            scratch_shapes=[pltpu.VMEM((dk, dv), F32)],
        ),
        out_shape=[jax.ShapeDtypeStruct((m, heads * dv), BF16),
                   jax.ShapeDtypeStruct((k_out, d_out), BF16)],
        compiler_params=_cparams(("parallel", "parallel", "arbitrary")),
        name="retention",
    )(log_gamma, proj, proj, proj, proj, w_out)


def kernel(x, fox_w_in, fox_b_f, fox_q_gain, fox_k_gain, fox_w_out, ret_w_in, ret_w_out,
           ln_gain, ln_bias):
    batch, seq, d_model = x.shape
    depth = ln_gain.shape[0]
    m = batch * seq
    alpha = (2.0 * depth) ** 0.25

    fox_width = fox_w_out.shape[1]
    ret_v_width = ret_w_out.shape[1]
    ret_heads = ret_v_width // RET_V_DIM
    ret_qk_width = ret_heads * RET_QK_DIM

    inv_freq = ROPE_BASE ** (-jnp.arange(0, RET_QK_DIM, 2, dtype=F32) / RET_QK_DIM)
    ang = jnp.arange(seq, dtype=F32)[:, None] * inv_freq[None, :]
    k_scale = RET_QK_DIM ** -0.5
    cos2 = jnp.stack([jnp.cos(ang), jnp.cos(ang) * k_scale])
    sin2 = jnp.stack([jnp.sin(ang), jnp.sin(ang) * k_scale])
    log_gamma = jnp.log1p(-jnp.exp2(-5.0 - jnp.arange(ret_heads, dtype=F32)))

    fox_w_in_t = jnp.swapaxes(fox_w_in, 1, 2)
    h = x.reshape(m, d_model)
    h_bf = None
    for i in range(depth):
        j = i // 2
        if i % 2 == 0:
            if h_bf is None:
                c3, h_bf = _fox_forget_cumsum(h, fox_w_in_t, j, fox_b_f[j], batch, seq,
                                              4 * fox_width)
                proj = _fox_in_proj(h_bf, fox_w_in_t, j, fox_q_gain[j], fox_k_gain[j], fox_width)
            else:
                proj = _fox_in_proj(h_bf, fox_w_in_t, j, fox_q_gain[j], fox_k_gain[j], fox_width)
                c3 = _fox_forget_cumsum(h_bf, fox_w_in_t, j, fox_b_f[j], batch, seq,
                                        4 * fox_width)
            y, w_out = _fox_attention(proj, c3, fox_w_out, j, batch, seq)
        else:
            proj = _ret_in_proj(h_bf, ret_w_in, j, cos2, sin2, seq, ret_qk_width, ret_v_width)
            y, w_out = _retention(proj, log_gamma, ret_w_out, j, batch, seq, ret_heads)
        h, h_bf = _out_proj_ln(y, w_out, h, ln_gain[i], ln_bias[i], alpha)
    return h.reshape(batch, seq, d_model)
```
